```python
import jax, jax.numpy as jnp
from jax import lax
import numpy as np

D_MODEL = 2048
BATCH = 1
SEQ = 8192
DEPTH = 4

CHUNK = 64
N_MIXERS = 2
HEAD_DIM = 128
N_HEADS = D_MODEL // HEAD_DIM
Q_BLOCK = 128
POOL_WINDOWS = (2, 4, 8, 16)
N_POOL_GROUPS = len(POOL_WINDOWS)
POOL_GROUP = D_MODEL // N_POOL_GROUPS
D_FF = -(-8 * D_MODEL // (3 * 256)) * 256
N_FOX_LAYERS = (DEPTH + N_MIXERS - 1) // N_MIXERS
N_POOL_LAYERS = DEPTH // N_MIXERS
RMS_EPS = 1e-6
NEG_INF = -1e30

kernel_name = "fox_pool_hybrid_encoder"


def rmsnorm(x, g):
    xf = x.astype(jnp.float32)
    y = xf * lax.rsqrt(jnp.mean(xf * xf, axis=-1, keepdims=True) + RMS_EPS)
    return (y * g.astype(jnp.float32)).astype(x.dtype)


def forgetting_attention(h, w_in, b_f, g_q, g_k, w_out):
    B, S, D = h.shape
    proj = jnp.einsum('bsd,de->bse', h, w_in)
    q = proj[..., :D].reshape(B, S, N_HEADS, HEAD_DIM)
    k = proj[..., D:2 * D].reshape(B, S, N_HEADS, HEAD_DIM)
    v = proj[..., 2 * D:3 * D].reshape(B, S, N_HEADS, HEAD_DIM)
    f_logit = proj[..., 3 * D:]
    q = rmsnorm(q, g_q)
    k = rmsnorm(k, g_k)
    log_f = jax.nn.log_sigmoid((f_logit + b_f).astype(jnp.float32))
    c = jnp.cumsum(log_f, axis=1).transpose(0, 2, 1)
    nb = S // Q_BLOCK
    q_blocks = q.reshape(B, nb, Q_BLOCK, N_HEADS, HEAD_DIM).transpose(1, 0, 2, 3, 4)
    c_blocks = c.reshape(B, N_HEADS, nb, Q_BLOCK).transpose(2, 0, 1, 3)
    starts = jnp.arange(nb) * Q_BLOCK
    k_pos = jnp.arange(S)
    scale = HEAD_DIM ** -0.5

    def block(args):
        qb, cb, start = args
        s = jnp.einsum('bqhd,bkhd->bhqk', qb, k).astype(jnp.float32) * scale
        s = s + (cb[..., :, None] - c[..., None, :])
        q_pos = start + jnp.arange(Q_BLOCK)
        s = jnp.where(q_pos[:, None] >= k_pos[None, :], s, NEG_INF)
        p = jax.nn.softmax(s, axis=-1).astype(v.dtype)
        return jnp.einsum('bhqk,bkhd->bqhd', p, v)

    o = lax.map(block, (q_blocks, c_blocks, starts))
    o = o.transpose(1, 0, 2, 3, 4).reshape(B, S, D)
    return jnp.einsum('bsd,de->bse', o, w_out)


def multiscale_pool(h, w, b, scale):
    B, S, D = h.shape
    hf = h.astype(jnp.float32)
    cs = jnp.concatenate([jnp.zeros((B, 1, D), jnp.float32), jnp.cumsum(hf, axis=1)], axis=1)
    t = jnp.arange(S)
    means = []
    for g, win in enumerate(POOL_WINDOWS):
        csg = cs[..., g * POOL_GROUP:(g + 1) * POOL_GROUP]
        lo = jnp.maximum(t + 1 - win, 0)
        cnt = (t + 1 - lo).astype(jnp.float32)
        means.append((csg[:, t + 1] - csg[:, lo]) / cnt[None, :, None])
    y = (jnp.concatenate(means, axis=-1) - hf).astype(h.dtype)
    y = jnp.einsum('bsgc,gce->bsge', y.reshape(B, S, N_POOL_GROUPS, POOL_GROUP), w).reshape(B, S, D)
    return (y + b) * scale


def swiglu(h, w_gu, w_down):
    gu = jnp.einsum('bsd,df->bsf', h, w_gu)
    gate, up = gu[..., :D_FF], gu[..., D_FF:]
    return jnp.einsum('bsf,fd->bsd', jax.nn.silu(gate) * up, w_down)


def setup_inputs(seed: int = 0) -> dict:
    key = jax.random.key(seed)
    ks = jax.random.split(key, 13)
    f32 = jnp.float32
    D = D_MODEL
    x = jax.random.normal(ks[0], (BATCH, SEQ, D), f32)
    mix_norm_g = 1.0 + 0.02 * jax.random.normal(ks[1], (DEPTH, D), f32)
    ffn_norm_g = 1.0 + 0.02 * jax.random.normal(ks[2], (DEPTH, D), f32)
    fox_w_in = jax.random.normal(ks[3], (N_FOX_LAYERS, D, 3 * D + N_HEADS), f32) * D ** -0.5
    fox_b_f = 2.0 + 0.5 * jax.random.normal(ks[4], (N_FOX_LAYERS, N_HEADS), f32)
    fox_q_norm_g = 1.0 + 0.02 * jax.random.normal(ks[5], (N_FOX_LAYERS, HEAD_DIM), f32)
    fox_k_norm_g = 1.0 + 0.02 * jax.random.normal(ks[6], (N_FOX_LAYERS, HEAD_DIM), f32)
    fox_w_out = jax.random.normal(ks[7], (N_FOX_LAYERS, D, D), f32) * D ** -0.5
    pool_w = jax.random.normal(ks[8], (N_POOL_LAYERS, N_POOL_GROUPS, POOL_GROUP, POOL_GROUP), f32) * POOL_GROUP ** -0.5
    pool_b = 0.01 * jax.random.normal(ks[9], (N_POOL_LAYERS, D), f32)
    pool_scale = 1.0 + 0.02 * jax.random.normal(ks[10], (N_POOL_LAYERS, D), f32)
    ffn_w_gate_up = jax.random.normal(ks[11], (DEPTH, D, 2 * D_FF), f32) * D ** -0.5
    ffn_w_down = jax.random.normal(ks[12], (DEPTH, D_FF, D), f32) * D_FF ** -0.5
    return {"x": x, "mix_norm_g": mix_norm_g, "ffn_norm_g": ffn_norm_g,
            "fox_w_in": fox_w_in, "fox_b_f": fox_b_f, "fox_q_norm_g": fox_q_norm_g,
            "fox_k_norm_g": fox_k_norm_g, "fox_w_out": fox_w_out,
            "pool_w": pool_w, "pool_b": pool_b, "pool_scale": pool_scale,
            "ffn_w_gate_up": ffn_w_gate_up, "ffn_w_down": ffn_w_down}


def reference(x, mix_norm_g, ffn_norm_g, fox_w_in, fox_b_f, fox_q_norm_g, fox_k_norm_g,
              fox_w_out, pool_w, pool_b, pool_scale, ffn_w_gate_up, ffn_w_down):
    for i in range(DEPTH):
        j = i // N_MIXERS
        h = rmsnorm(x, mix_norm_g[i])
        if i % N_MIXERS == 0:
            x = x + forgetting_attention(h, fox_w_in[j], fox_b_f[j], fox_q_norm_g[j],
                                         fox_k_norm_g[j], fox_w_out[j])
        else:
            x = x + multiscale_pool(h, pool_w[j], pool_b[j], pool_scale[j])
        h = rmsnorm(x, ffn_norm_g[i])
        x = x + swiglu(h, ffn_w_gate_up[i], ffn_w_down[i])
    return x
```

```python
import functools
import math

import jax
import jax.numpy as jnp
from jax import lax
from jax.experimental import pallas as pl
from jax.experimental.pallas import tpu as pltpu

D_MODEL = 2048
SEQ = 8192
DEPTH = 4
HEAD_DIM = 128
N_HEADS = D_MODEL // HEAD_DIM
POOL_WINDOWS = (2, 4, 8, 16)
POOL_GROUP = D_MODEL // len(POOL_WINDOWS)
POOL_HALO = max(POOL_WINDOWS)
D_FF = 5632
RMS_EPS = 1e-6
NEG_INF = -1e30
LOG2E = math.log2(math.e)

LANES = 128
VMEM_LIMIT = 56 * 1024 * 1024

F32 = jnp.float32
BF16 = jnp.bfloat16


def _params(semantics):
    return pltpu.CompilerParams(dimension_semantics=semantics,
                                vmem_limit_bytes=VMEM_LIMIT)


def _rmsnorm(x, g):
    return x * lax.rsqrt(jnp.mean(x * x, axis=-1, keepdims=True) + RMS_EPS) * g


def _split3_bf16(x):
    x1 = x.astype(BF16)
    r1 = x - x1.astype(F32)
    x2 = r1.astype(BF16)
    x3 = (r1 - x2.astype(F32)).astype(BF16)
    return x1, x2, x3


def _fox_proj_kernel(x_ref, g_ref, w_ref, gain_ref, wf_ref, bf_ref, tri_ref,
                     qkv_ref, ct_ref, h_scr, carry_scr, *, tm, tn):
    i = pl.program_id(0)
    j = pl.program_id(1)

    @pl.when(j == 0)
    def _():
        hb = _rmsnorm(x_ref[...], g_ref[...]).astype(BF16)
        h_scr[...] = hb
        f = jnp.dot(hb, wf_ref[...], preferred_element_type=F32) + bf_ref[...]
        logf = jnp.minimum(f, 0.0) - jnp.log1p(jnp.exp(-jnp.abs(f)))

        @pl.when(i == 0)
        def _():
            carry_scr[...] = jnp.zeros_like(carry_scr)

        tri = tri_ref[...]
        carry = carry_scr[0:1, :]
        chunks = []
        for r in range(tm // LANES):
            x1, x2, x3 = _split3_bf16(logf[r * LANES:(r + 1) * LANES, :])
            cs = (jnp.dot(tri, x1, preferred_element_type=F32)
                  + jnp.dot(tri, x2, preferred_element_type=F32)
                  + jnp.dot(tri, x3, preferred_element_type=F32)) + carry
            carry = cs[LANES - 1:LANES, :]
            chunks.append(cs)
        carry_scr[0:1, :] = carry
        c = jnp.concatenate(chunks, axis=0) * LOG2E
        ct_ref[...] = c.T[:N_HEADS, :]

    r = jnp.dot(h_scr[...], w_ref[...], preferred_element_type=F32)

    @pl.when(j < 2 * D_MODEL // tn)
    def _():
        for hh in range(tn // HEAD_DIM):
            sl = slice(hh * HEAD_DIM, (hh + 1) * HEAD_DIM)
            qkv_ref[:, sl] = _rmsnorm(r[:, sl], gain_ref[:, sl]).astype(BF16)

    @pl.when(j >= 2 * D_MODEL // tn)
    def _():
        qkv_ref[...] = r.astype(BF16)


def _fox_proj(x, g, w_qkv, gains, wf, bf, tri, *, tm=1024, tn=512):
    n = 3 * D_MODEL
    return pl.pallas_call(
        functools.partial(_fox_proj_kernel, tm=tm, tn=tn),
        grid=(SEQ // tm, n // tn),
        in_specs=[
            pl.BlockSpec((tm, D_MODEL), lambda i, j: (i, 0)),
            pl.BlockSpec((1, D_MODEL), lambda i, j: (0, 0)),
            pl.BlockSpec((D_MODEL, tn), lambda i, j: (0, j)),
            pl.BlockSpec((1, tn), lambda i, j: (0, j)),
            pl.BlockSpec((D_MODEL, LANES), lambda i, j: (0, 0)),
            pl.BlockSpec((1, LANES), lambda i, j: (0, 0)),
            pl.BlockSpec((LANES, LANES), lambda i, j: (0, 0)),
        ],
        out_specs=[
            pl.BlockSpec((tm, tn), lambda i, j: (i, j)),
            pl.BlockSpec((N_HEADS, tm), lambda i, j: (0, i)),
        ],
        out_shape=[
            jax.ShapeDtypeStruct((SEQ, n), BF16),
            jax.ShapeDtypeStruct((N_HEADS, SEQ), F32),
        ],
        scratch_shapes=[
            pltpu.VMEM((tm, D_MODEL), BF16),
            pltpu.VMEM((8, LANES), F32),
        ],
        compiler_params=_params(("arbitrary", "arbitrary")),
        name="fox_proj",
    )(x, g, w_qkv, gains, wf, bf, tri)


def _fox_attn_kernel(q_ref, k_ref, v_ref, c_ref, o_ref, m_scr, l_scr, acc_scr, *, t):
    qi = pl.program_id(1)
    q = q_ref[...]
    m_scr[...] = jnp.full_like(m_scr, NEG_INF)
    l_scr[...] = jnp.zeros_like(l_scr)
    acc_scr[...] = jnp.zeros_like(acc_scr)
    q0 = pl.multiple_of(qi * t, t)
    c_q0 = c_ref[0, :, pl.ds(q0, LANES)][:, 0:1]

    def step(j, masked):
        k0 = pl.multiple_of(j * t, t)
        k = k_ref[pl.ds(k0, t), :]
        v = v_ref[pl.ds(k0, t), :]
        s = lax.dot_general(q, k, (((1,), (1,)), ((), ())),
                            preferred_element_type=F32)
        s = s + (c_q0 - c_ref[0, :, pl.ds(k0, t)])
        if masked:
            row = lax.broadcasted_iota(jnp.int32, (t, t), 0)
            col = lax.broadcasted_iota(jnp.int32, (t, t), 1)
            s = jnp.where(row >= col, s, NEG_INF)
        m_prev = m_scr[...]
        m_next = jnp.maximum(m_prev, jnp.max(s, axis=1, keepdims=True))
        p = jnp.exp2(s - pltpu.repeat(m_next, t // LANES, axis=1))
        alpha = jnp.exp2(m_prev - m_next)
        l_scr[...] = alpha * l_scr[...] + jnp.sum(p, axis=1, keepdims=True)
        m_scr[...] = m_next
        acc_scr[...] = alpha * acc_scr[...] + jnp.dot(
            p.astype(BF16), v, preferred_element_type=F32)

    def body(j, carry):
        step(j, False)
        return carry

    lax.fori_loop(0, qi, body, 0)
    step(qi, True)
    o_ref[...] = (acc_scr[...] / l_scr[...]).astype(BF16)


def _fox_attn(qkv, c3, *, t=512):
    return pl.pallas_call(
        functools.partial(_fox_attn_kernel, t=t),
        grid=(N_HEADS, SEQ // t),
        in_specs=[
            pl.BlockSpec((t, HEAD_DIM), lambda h, qi: (qi, h)),
            pl.BlockSpec((SEQ, HEAD_DIM), lambda h, qi: (0, N_HEADS + h)),
            pl.BlockSpec((SEQ, HEAD_DIM), lambda h, qi: (0, 2 * N_HEADS + h)),
            pl.BlockSpec((1, 1, SEQ), lambda h, qi: (h, 0, 0)),
        ],
        out_specs=pl.BlockSpec((t, HEAD_DIM), lambda h, qi: (qi, h)),
        out_shape=jax.ShapeDtypeStruct((SEQ, D_MODEL), BF16),
        scratch_shapes=[
            pltpu.VMEM((t, LANES), F32),
            pltpu.VMEM((t, LANES), F32),
            pltpu.VMEM((t, HEAD_DIM), F32),
        ],
        compiler_params=_params(("arbitrary", "arbitrary")),
        name="fox_attn",
    )(qkv, qkv, qkv, c3)


def _fox_out_kernel(x_ref, o_ref, w_ref, y_ref):
    y_ref[...] = x_ref[...] + jnp.dot(o_ref[...], w_ref[...],
                                      preferred_element_type=F32)


def _fox_out(x, o, w_out, *, tm=512):
    return pl.pallas_call(
        _fox_out_kernel,
        grid=(SEQ // tm,),
        in_specs=[
            pl.BlockSpec((tm, D_MODEL), lambda i: (i, 0)),
            pl.BlockSpec((tm, D_MODEL), lambda i: (i, 0)),
            pl.BlockSpec((D_MODEL, D_MODEL), lambda i: (0, 0)),
        ],
        out_specs=pl.BlockSpec((tm, D_MODEL), lambda i: (i, 0)),
        out_shape=jax.ShapeDtypeStruct((SEQ, D_MODEL), F32),
        compiler_params=_params(("arbitrary",)),
        name="fox_out",
    )(x, o, w_out)


def _pool_kernel(x_ref, g_ref, w_ref, b_ref, s_ref, y_ref, h_scr, *, tm):
    i = pl.program_id(0)

    @pl.when(i == 0)
    def _():
        h_scr[0:POOL_HALO, :] = jnp.zeros((POOL_HALO, D_MODEL), F32)

    x = x_ref[...]
    h_scr[POOL_HALO:POOL_HALO + tm, :] = _rmsnorm(x, g_ref[...])
    t = i * tm + lax.broadcasted_iota(jnp.int32, (tm, 1), 0)
    for gi, win in enumerate(POOL_WINDOWS):
        cols = slice(gi * POOL_GROUP, (gi + 1) * POOL_GROUP)
        h = h_scr[POOL_HALO:POOL_HALO + tm, cols]
        tot = h
        for d in range(1, win):
            tot = tot + h_scr[POOL_HALO - d:POOL_HALO - d + tm, cols]
        cnt = jnp.minimum(t + 1, win).astype(F32)
        y = (tot / cnt - h).astype(BF16)
        y = jnp.dot(y, w_ref[gi], preferred_element_type=F32)
        y_ref[:, cols] = x[:, cols] + (y + b_ref[:, cols]) * s_ref[:, cols]
    h_scr[0:POOL_HALO, :] = h_scr[tm:tm + POOL_HALO, :]


def _pool_mix(x, g, w, b, scale, *, tm=512):
    n_groups = len(POOL_WINDOWS)
    return pl.pallas_call(
        functools.partial(_pool_kernel, tm=tm),
        grid=(SEQ // tm,),
        in_specs=[
            pl.BlockSpec((tm, D_MODEL), lambda i: (i, 0)),
            pl.BlockSpec((1, D_MODEL), lambda i: (0, 0)),
            pl.BlockSpec((n_groups, POOL_GROUP, POOL_GROUP), lambda i: (0, 0, 0)),
            pl.BlockSpec((1, D_MODEL), lambda i: (0, 0)),
            pl.BlockSpec((1, D_MODEL), lambda i: (0, 0)),
        ],
        out_specs=pl.BlockSpec((tm, D_MODEL), lambda i: (i, 0)),
        out_shape=jax.ShapeDtypeStruct((SEQ, D_MODEL), F32),
        scratch_shapes=[pltpu.VMEM((POOL_HALO + tm, D_MODEL), F32)],
        compiler_params=_params(("arbitrary",)),
        name="pool_mix",
    )(x, g, w, b, scale)


def _ffn_kernel(x_ref, g_ref, wg_ref, wu_ref, wd_ref, y_ref, h_scr):
    j = pl.program_id(1)

    @pl.when(j == 0)
    def _():
        x = x_ref[...]
        h_scr[...] = _rmsnorm(x, g_ref[...]).astype(BF16)
        y_ref[...] = x

    h = h_scr[...]
    gate = jnp.dot(h, wg_ref[...], preferred_element_type=F32)
    up = jnp.dot(h, wu_ref[...], preferred_element_type=F32)
    a = (gate * jax.nn.sigmoid(gate) * up).astype(BF16)
    y_ref[...] += jnp.dot(a, wd_ref[...], preferred_element_type=F32)


def _ffn(x, g, w_gu, w_down, *, tm=512, tf=512):
    nf = D_FF // tf
    return pl.pallas_call(
        _ffn_kernel,
        grid=(SEQ // tm, nf),
        in_specs=[
            pl.BlockSpec((tm, D_MODEL), lambda i, j: (i, 0)),
            pl.BlockSpec((1, D_MODEL), lambda i, j: (0, 0)),
            pl.BlockSpec((D_MODEL, tf), lambda i, j: (0, j)),
            pl.BlockSpec((D_MODEL, tf), lambda i, j: (0, nf + j)),
            pl.BlockSpec((tf, D_MODEL), lambda i, j: (j, 0)),
        ],
        out_specs=pl.BlockSpec((tm, D_MODEL), lambda i, j: (i, 0)),
        out_shape=jax.ShapeDtypeStruct((SEQ, D_MODEL), F32),
        scratch_shapes=[pltpu.VMEM((tm, D_MODEL), BF16)],
        compiler_params=_params(("arbitrary", "arbitrary")),
        name="ffn",
    )(x, g, w_gu, w_gu, w_down)


def kernel(x, mix_norm_g, ffn_norm_g, fox_w_in, fox_b_f, fox_q_norm_g, fox_k_norm_g,
           fox_w_out, pool_w, pool_b, pool_scale, ffn_w_gate_up, ffn_w_down):
    assert x.shape == (1, SEQ, D_MODEL)
    x = x[0]
    tri = jnp.tril(jnp.ones((LANES, LANES), BF16))
    q_scale = HEAD_DIM ** -0.5 * LOG2E
    for i in range(DEPTH):
        j = i // 2
        g_mix = mix_norm_g[i][None, :]
        if i % 2 == 0:
            w_in = fox_w_in[j]
            w_qkv = w_in[:, :3 * D_MODEL].astype(BF16)
            wf = jnp.pad(w_in[:, 3 * D_MODEL:], ((0, 0), (0, LANES - N_HEADS))).astype(BF16)
            bf = jnp.pad(fox_b_f[j], (0, LANES - N_HEADS))[None, :]
            gains = jnp.concatenate([
                jnp.tile(fox_q_norm_g[j] * q_scale, N_HEADS),
                jnp.tile(fox_k_norm_g[j], N_HEADS),
                jnp.ones((D_MODEL,), F32)])[None, :]
            qkv, ct = _fox_proj(x, g_mix, w_qkv, gains, wf, bf, tri)
            o = _fox_attn(qkv, ct[:, None, :])
            x = _fox_out(x, o, fox_w_out[j].astype(BF16))
        else:
            x = _pool_mix(x, g_mix, pool_w[j].astype(BF16), pool_b[j][None, :],
                          pool_scale[j][None, :])
        x = _ffn(x, ffn_norm_g[i][None, :], ffn_w_gate_up[i].astype(BF16),
                 ffn_w_down[i].astype(BF16))
    return x[None]
```

```python
import functools
import math

import jax
import jax.numpy as jnp
from jax import lax
from jax.experimental import pallas as pl
from jax.experimental.pallas import tpu as pltpu

D_MODEL = 2048
SEQ = 8192
DEPTH = 4
HEAD_DIM = 128
N_HEADS = D_MODEL // HEAD_DIM
POOL_WINDOWS = (2, 4, 8, 16)
POOL_GROUP = D_MODEL // len(POOL_WINDOWS)
POOL_HALO = max(POOL_WINDOWS)
D_FF = 5632
RMS_EPS = 1e-6
NEG_INF = -1e30
LOG2E = math.log2(math.e)

LANES = 128
VMEM_LIMIT = 56 * 1024 * 1024

F32 = jnp.float32
BF16 = jnp.bfloat16


def _params(semantics):
    return pltpu.CompilerParams(dimension_semantics=semantics,
                                vmem_limit_bytes=VMEM_LIMIT)


def _rmsnorm(x, g):
    return x * lax.rsqrt(jnp.mean(x * x, axis=-1, keepdims=True) + RMS_EPS) * g


def _split3_bf16(x):
    x1 = x.astype(BF16)
    r1 = x - x1.astype(F32)
    x2 = r1.astype(BF16)
    x3 = (r1 - x2.astype(F32)).astype(BF16)
    return x1, x2, x3


def _fox_proj_kernel(x_ref, g_ref, w_ref, gain_ref, wf_ref, bf_ref, tri_ref,
                     qkv_ref, ct_ref, h_scr, carry_scr, *, tm, tn):
    i = pl.program_id(0)
    j = pl.program_id(1)

    @pl.when(j == 0)
    def _():
        hb = _rmsnorm(x_ref[...], g_ref[...]).astype(BF16)
        h_scr[...] = hb
        f = jnp.dot(hb, wf_ref[...], preferred_element_type=F32) + bf_ref[...]
        logf = jnp.minimum(f, 0.0) - jnp.log1p(jnp.exp(-jnp.abs(f)))

        @pl.when(i == 0)
        def _():
            carry_scr[...] = jnp.zeros_like(carry_scr)

        tri = tri_ref[...]
        carry = carry_scr[0:1, :]
        chunks = []
        for r in range(tm // LANES):
            x1, x2, x3 = _split3_bf16(logf[r * LANES:(r + 1) * LANES, :])
            cs = (jnp.dot(tri, x1, preferred_element_type=F32)
                  + jnp.dot(tri, x2, preferred_element_type=F32)
                  + jnp.dot(tri, x3, preferred_element_type=F32)) + carry
            carry = cs[LANES - 1:LANES, :]
            chunks.append(cs)
        carry_scr[0:1, :] = carry
        c = jnp.concatenate(chunks, axis=0) * LOG2E
        ct_ref[...] = c.T[:N_HEADS, :]

    r = jnp.dot(h_scr[...], w_ref[...].astype(BF16),
                preferred_element_type=F32)

    @pl.when(j < 2 * D_MODEL // tn)
    def _():
        for hh in range(tn // HEAD_DIM):
            sl = slice(hh * HEAD_DIM, (hh + 1) * HEAD_DIM)
            qkv_ref[:, sl] = _rmsnorm(r[:, sl], gain_ref[:, sl]).astype(BF16)

    @pl.when(j >= 2 * D_MODEL // tn)
    def _():
        qkv_ref[...] = r.astype(BF16)


def _fox_proj(x, g, w_in, layer, gains, wf, bf, tri, *, tm=1024, tn=512):
    n = 3 * D_MODEL
    return pl.pallas_call(
        functools.partial(_fox_proj_kernel, tm=tm, tn=tn),
        grid=(SEQ // tm, n // tn),
        in_specs=[
            pl.BlockSpec((tm, D_MODEL), lambda i, j: (i, 0)),
            pl.BlockSpec((1, D_MODEL), lambda i, j: (0, 0)),
            pl.BlockSpec((None, D_MODEL, tn), lambda i, j: (layer, 0, j)),
            pl.BlockSpec((1, tn), lambda i, j: (0, j)),
            pl.BlockSpec((D_MODEL, LANES), lambda i, j: (0, 0)),
            pl.BlockSpec((1, LANES), lambda i, j: (0, 0)),
            pl.BlockSpec((LANES, LANES), lambda i, j: (0, 0)),
        ],
        out_specs=[
            pl.BlockSpec((tm, tn), lambda i, j: (i, j)),
            pl.BlockSpec((N_HEADS, tm), lambda i, j: (0, i)),
        ],
        out_shape=[
            jax.ShapeDtypeStruct((SEQ, n), BF16),
            jax.ShapeDtypeStruct((N_HEADS, SEQ), F32),
        ],
        scratch_shapes=[
            pltpu.VMEM((tm, D_MODEL), BF16),
            pltpu.VMEM((8, LANES), F32),
        ],
        compiler_params=_params(("arbitrary", "arbitrary")),
        name="fox_proj",
    )(x, g, w_in, gains, wf, bf, tri)


SKIP_EXP2 = 160.0
ATTN_BLOCK = 512


def _fox_attn_kernel(thr_ref, q_ref, k_ref, v_ref, c_ref, cend_ref, o_ref,
                     m_scr, l_scr, acc_scr, *, t):
    qi = pl.program_id(1)
    m_scr[...] = jnp.full_like(m_scr, NEG_INF)
    l_scr[...] = jnp.zeros_like(l_scr)
    acc_scr[...] = jnp.zeros_like(acc_scr)
    q0 = pl.multiple_of(qi * t, t)
    c_q0 = c_ref[0, :, pl.ds(q0, LANES)][:, 0:1]

    blk = lax.broadcasted_iota(jnp.int32, (1, LANES), 1)
    skippable = ((c_q0 - cend_ref[0]) < thr_ref[...]) & (blk < qi)
    first = jnp.sum(skippable.astype(jnp.int32))

    def step(j, masked):
        k0 = pl.multiple_of(j * t, t)
        s = lax.dot_general(q_ref[...], k_ref[pl.ds(k0, t), :], (((1,), (1,)), ((), ())),
                            preferred_element_type=F32)
        s = s + (c_q0 - c_ref[0, :, pl.ds(k0, t)])
        if masked:
            row = lax.broadcasted_iota(jnp.int32, (t, t), 0)
            col = lax.broadcasted_iota(jnp.int32, (t, t), 1)
            s = jnp.where(row >= col, s, NEG_INF)
        m_prev = m_scr[...]
        m_next = jnp.maximum(m_prev, jnp.max(s, axis=1, keepdims=True))
        p = jnp.exp2(s - pltpu.repeat(m_next, t // LANES, axis=1))
        alpha = jnp.exp2(m_prev - m_next)
        l_scr[...] = alpha * l_scr[...] + jnp.sum(p, axis=1, keepdims=True)
        m_scr[...] = m_next
        acc_scr[...] = alpha * acc_scr[...] + jnp.dot(
            p.astype(BF16), v_ref[pl.ds(k0, t), :], preferred_element_type=F32)

    def body(j, carry):
        step(j, False)
        return carry

    lax.fori_loop(first, qi, body, 0)
    step(qi, True)
    o_ref[...] = (acc_scr[...] / l_scr[...]).astype(BF16)


def _fox_attn(qkv, c3, cend3, thr, *, t=512):
    return pl.pallas_call(
        functools.partial(_fox_attn_kernel, t=t),
        grid=(N_HEADS, SEQ // t),
        in_specs=[
            pl.BlockSpec((1, 1), lambda h, qi: (0, 0)),
            pl.BlockSpec((t, HEAD_DIM), lambda h, qi: (qi, h)),
            pl.BlockSpec((SEQ, HEAD_DIM), lambda h, qi: (0, N_HEADS + h)),
            pl.BlockSpec((SEQ, HEAD_DIM), lambda h, qi: (0, 2 * N_HEADS + h)),
            pl.BlockSpec((1, 1, SEQ), lambda h, qi: (h, 0, 0)),
            pl.BlockSpec((1, 1, LANES), lambda h, qi: (h, 0, 0)),
        ],
        out_specs=pl.BlockSpec((t, HEAD_DIM), lambda h, qi: (qi, h)),
        out_shape=jax.ShapeDtypeStruct((SEQ, D_MODEL), BF16),
        scratch_shapes=[
            pltpu.VMEM((t, LANES), F32),
            pltpu.VMEM((t, LANES), F32),
            pltpu.VMEM((t, HEAD_DIM), F32),
        ],
        compiler_params=_params(("arbitrary", "arbitrary")),
        name="fox_attn",
    )(thr, qkv, qkv, qkv, c3, cend3)


def _fox_out_kernel(x_ref, o_ref, w_ref, y_ref, wb_scr):
    @pl.when(pl.program_id(0) == 0)
    def _():
        wb_scr[...] = w_ref[...].astype(BF16)

    y_ref[...] = x_ref[...] + jnp.dot(o_ref[...], wb_scr[...],
                                      preferred_element_type=F32)


def _fox_out(x, o, w_out, layer, *, tm=512):
    return pl.pallas_call(
        _fox_out_kernel,
        grid=(SEQ // tm,),
        in_specs=[
            pl.BlockSpec((tm, D_MODEL), lambda i: (i, 0)),
            pl.BlockSpec((tm, D_MODEL), lambda i: (i, 0)),
            pl.BlockSpec((None, D_MODEL, D_MODEL), lambda i: (layer, 0, 0),
                         pipeline_mode=pl.Buffered(1)),
        ],
        out_specs=pl.BlockSpec((tm, D_MODEL), lambda i: (i, 0)),
        out_shape=jax.ShapeDtypeStruct((SEQ, D_MODEL), F32),
        scratch_shapes=[pltpu.VMEM((D_MODEL, D_MODEL), BF16)],
        compiler_params=_params(("arbitrary",)),
        name="fox_out",
    )(x, o, w_out)


def _pool_kernel(x_ref, g_ref, w_ref, b_ref, s_ref, y_ref, h_scr, *, tm):
    i = pl.program_id(0)

    @pl.when(i == 0)
    def _():
        h_scr[0:POOL_HALO, :] = jnp.zeros((POOL_HALO, D_MODEL), F32)

    x = x_ref[...]
    h_scr[POOL_HALO:POOL_HALO + tm, :] = _rmsnorm(x, g_ref[...])
    t = i * tm + lax.broadcasted_iota(jnp.int32, (tm, 1), 0)
    for gi, win in enumerate(POOL_WINDOWS):
        cols = slice(gi * POOL_GROUP, (gi + 1) * POOL_GROUP)
        h = h_scr[POOL_HALO:POOL_HALO + tm, cols]
        tot = h
        for d in range(1, win):
            tot = tot + h_scr[POOL_HALO - d:POOL_HALO - d + tm, cols]
        cnt = jnp.minimum(t + 1, win).astype(F32)
        y = (tot / cnt - h).astype(BF16)
        y = jnp.dot(y, w_ref[gi], preferred_element_type=F32)
        y_ref[:, cols] = x[:, cols] + (y + b_ref[:, cols]) * s_ref[:, cols]
    h_scr[0:POOL_HALO, :] = h_scr[tm:tm + POOL_HALO, :]


def _pool_mix(x, g, w, b, scale, *, tm=512):
    n_groups = len(POOL_WINDOWS)
    return pl.pallas_call(
        functools.partial(_pool_kernel, tm=tm),
        grid=(SEQ // tm,),
        in_specs=[
            pl.BlockSpec((tm, D_MODEL), lambda i: (i, 0)),
            pl.BlockSpec((1, D_MODEL), lambda i: (0, 0)),
            pl.BlockSpec((n_groups, POOL_GROUP, POOL_GROUP), lambda i: (0, 0, 0)),
            pl.BlockSpec((1, D_MODEL), lambda i: (0, 0)),
            pl.BlockSpec((1, D_MODEL), lambda i: (0, 0)),
        ],
        out_specs=pl.BlockSpec((tm, D_MODEL), lambda i: (i, 0)),
        out_shape=jax.ShapeDtypeStruct((SEQ, D_MODEL), F32),
        scratch_shapes=[pltpu.VMEM((POOL_HALO + tm, D_MODEL), F32)],
        compiler_params=_params(("arbitrary",)),
        name="pool_mix",
    )(x, g, w, b, scale)


def _ffn_kernel(x_ref, g_ref, wg_ref, wu_ref, wd_ref, y_ref, h_scr):
    j = pl.program_id(1)

    @pl.when(j == 0)
    def _():
        x = x_ref[...]
        h_scr[...] = _rmsnorm(x, g_ref[...]).astype(BF16)
        y_ref[...] = x

    h = h_scr[...]
    gate = jnp.dot(h, wg_ref[...], preferred_element_type=F32)
    up = jnp.dot(h, wu_ref[...], preferred_element_type=F32)
    a = (gate * jax.nn.sigmoid(gate) * up).astype(BF16)
    y_ref[...] += jnp.dot(a, wd_ref[...], preferred_element_type=F32)


def _ffn(x, g, w_gu, w_down, *, tm=1024, tf=512):
    nf = D_FF // tf
    return pl.pallas_call(
        _ffn_kernel,
        grid=(SEQ // tm, nf),
        in_specs=[
            pl.BlockSpec((tm, D_MODEL), lambda i, j: (i, 0)),
            pl.BlockSpec((1, D_MODEL), lambda i, j: (0, 0)),
            pl.BlockSpec((D_MODEL, tf), lambda i, j: (0, j)),
            pl.BlockSpec((D_MODEL, tf), lambda i, j: (0, nf + j)),
            pl.BlockSpec((tf, D_MODEL), lambda i, j: (j, 0)),
        ],
        out_specs=pl.BlockSpec((tm, D_MODEL), lambda i, j: (i, 0)),
        out_shape=jax.ShapeDtypeStruct((SEQ, D_MODEL), F32),
        scratch_shapes=[pltpu.VMEM((tm, D_MODEL), BF16)],
        compiler_params=_params(("arbitrary", "arbitrary")),
        name="ffn",
    )(x, g, w_gu, w_gu, w_down)


def kernel(x, mix_norm_g, ffn_norm_g, fox_w_in, fox_b_f, fox_q_norm_g, fox_k_norm_g,
           fox_w_out, pool_w, pool_b, pool_scale, ffn_w_gate_up, ffn_w_down):
    assert x.shape == (1, SEQ, D_MODEL)
    x = x.reshape(SEQ, D_MODEL)
    tri = jnp.tril(jnp.ones((LANES, LANES), BF16))
    q_scale = HEAD_DIM ** -0.5 * LOG2E
    for i in range(DEPTH):
        j = i // 2
        g_mix = mix_norm_g[i][None, :]
        if i % 2 == 0:
            wf = jnp.pad(fox_w_in[j, :, 3 * D_MODEL:],
                         ((0, 0), (0, LANES - N_HEADS))).astype(BF16)
            bf = jnp.pad(fox_b_f[j], (0, LANES - N_HEADS))[None, :]
            gains = jnp.concatenate([
                jnp.tile(fox_q_norm_g[j] * q_scale, N_HEADS),
                jnp.tile(fox_k_norm_g[j], N_HEADS),
                jnp.ones((D_MODEL,), F32)])[None, :]
            qkv, ct = _fox_proj(x, g_mix, fox_w_in, j, gains, wf, bf, tri)
            qk_bound = (1.02 * HEAD_DIM * q_scale * jnp.max(jnp.abs(fox_q_norm_g[j]))
                        * jnp.max(jnp.abs(fox_k_norm_g[j])))
            thr = jnp.reshape(-(SKIP_EXP2 + 2.0 * qk_bound), (1, 1)).astype(F32)
            cend = jnp.pad(ct[:, ATTN_BLOCK - 1::ATTN_BLOCK],
                           ((0, 0), (0, LANES - SEQ // ATTN_BLOCK)))
            o = _fox_attn(qkv, ct[:, None, :], cend[:, None, :], thr, t=ATTN_BLOCK)
            x = _fox_out(x, o, fox_w_out, j)
        else:
            x = _pool_mix(x, g_mix, pool_w[j].astype(BF16), pool_b[j][None, :],
                          pool_scale[j][None, :])
        x = _ffn(x, ffn_norm_g[i][None, :], ffn_w_gate_up[i].astype(BF16),
                 ffn_w_down[i].astype(BF16))
    return x.reshape(1, SEQ, D_MODEL)
```

```python
import functools
import math

import jax
import jax.numpy as jnp
from jax import lax
from jax.experimental import pallas as pl
from jax.experimental.pallas import tpu as pltpu

D_MODEL = 2048
SEQ = 8192
DEPTH = 4
HEAD_DIM = 128
N_HEADS = D_MODEL // HEAD_DIM
POOL_WINDOWS = (2, 4, 8, 16)
POOL_GROUP = D_MODEL // len(POOL_WINDOWS)
POOL_HALO = max(POOL_WINDOWS)
D_FF = 5632
RMS_EPS = 1e-6
NEG_INF = -1e30
LOG2E = math.log2(math.e)

LANES = 128
VMEM_LIMIT = 62 * 1024 * 1024

F32 = jnp.float32
BF16 = jnp.bfloat16


def _params(semantics):
    return pltpu.CompilerParams(dimension_semantics=semantics,
                                vmem_limit_bytes=VMEM_LIMIT)


def _rmsnorm(x, g):
    return x * lax.rsqrt(jnp.mean(x * x, axis=-1, keepdims=True) + RMS_EPS) * g


def _split3_bf16(x):
    x1 = x.astype(BF16)
    r1 = x - x1.astype(F32)
    x2 = r1.astype(BF16)
    x3 = (r1 - x2.astype(F32)).astype(BF16)
    return x1, x2, x3


ROW_CHUNK = 256


def _fox_proj_kernel(x_ref, g_ref, w_ref, gain_ref, wf_ref, bf_ref, tri_ref,
                     qkv_ref, ct_ref, h_scr, carry_scr, *, tm, tn):
    i = pl.program_id(0)
    j = pl.program_id(1)

    @pl.when(j == 0)
    def _():
        hb = _rmsnorm(x_ref[...], g_ref[...]).astype(BF16)
        h_scr[...] = hb
        f = jnp.dot(hb, wf_ref[...].astype(BF16), preferred_element_type=F32) + bf_ref[...]
        logf = jnp.minimum(f, 0.0) - jnp.log1p(jnp.exp(-jnp.abs(f)))

        @pl.when(i == 0)
        def _():
            carry_scr[...] = jnp.zeros_like(carry_scr)

        tri = tri_ref[...]
        carry = carry_scr[0:1, :]
        chunks = []
        for r in range(tm // LANES):
            x1, x2, x3 = _split3_bf16(logf[r * LANES:(r + 1) * LANES, :])
            cs = (jnp.dot(tri, x1, preferred_element_type=F32)
                  + jnp.dot(tri, x2, preferred_element_type=F32)
                  + jnp.dot(tri, x3, preferred_element_type=F32)) + carry
            carry = cs[LANES - 1:LANES, :]
            chunks.append(cs)
        carry_scr[0:1, :] = carry
        c = jnp.concatenate(chunks, axis=0) * LOG2E
        ct_ref[...] = c.T[:N_HEADS, :]

    w = w_ref[...].astype(BF16)
    is_qk = (j < 2 * D_MODEL // tn).astype(F32)
    for rc in range(tm // ROW_CHUNK):
        rows = slice(rc * ROW_CHUNK, (rc + 1) * ROW_CHUNK)
        r = jnp.dot(h_scr[rows, :], w, preferred_element_type=F32)
        for hh in range(tn // HEAD_DIM):
            sl = slice(hh * HEAD_DIM, (hh + 1) * HEAD_DIM)
            rh = r[:, sl]
            inv = lax.rsqrt(jnp.mean(rh * rh, axis=-1, keepdims=True) + RMS_EPS)
            scale = is_qk * inv + (1.0 - is_qk)
            qkv_ref[rows, sl] = (rh * scale * gain_ref[:, sl]).astype(BF16)


def _fox_proj(x, g, w_in, layer, gains, wf, bf, tri, *, tm=1024, tn=1024):
    n = 3 * D_MODEL
    return pl.pallas_call(
        functools.partial(_fox_proj_kernel, tm=tm, tn=tn),
        grid=(SEQ // tm, n // tn),
        in_specs=[
            pl.BlockSpec((tm, D_MODEL), lambda i, j: (i, 0)),
            pl.BlockSpec((1, D_MODEL), lambda i, j: (0, 0)),
            pl.BlockSpec((None, D_MODEL, tn), lambda i, j: (layer, 0, j)),
            pl.BlockSpec((1, tn), lambda i, j: (0, j)),
            pl.BlockSpec((D_MODEL, LANES), lambda i, j: (0, 0)),
            pl.BlockSpec((1, LANES), lambda i, j: (0, 0)),
            pl.BlockSpec((LANES, LANES), lambda i, j: (0, 0)),
        ],
        out_specs=[
            pl.BlockSpec((tm, tn), lambda i, j: (i, j)),
            pl.BlockSpec((N_HEADS, tm), lambda i, j: (0, i)),
        ],
        out_shape=[
            jax.ShapeDtypeStruct((SEQ, n), BF16),
            jax.ShapeDtypeStruct((N_HEADS, SEQ), F32),
        ],
        scratch_shapes=[
            pltpu.VMEM((tm, D_MODEL), BF16),
            pltpu.VMEM((8, LANES), F32),
        ],
        compiler_params=_params(("arbitrary", "arbitrary")),
        name="fox_proj",
    )(x, g, w_in, gains, wf, bf, tri)


SKIP_EXP2 = 160.0
ATTN_BLOCK = 512


def _fox_attn_kernel(thr_ref, q_ref, k_ref, v_ref, c_ref, cend_ref, o_ref,
                     m_scr, l_scr, acc_scr, *, t):
    qi = pl.program_id(1)
    m_scr[...] = jnp.full_like(m_scr, NEG_INF)
    l_scr[...] = jnp.zeros_like(l_scr)
    acc_scr[...] = jnp.zeros_like(acc_scr)
    q0 = pl.multiple_of(qi * t, t)
    c_q0 = c_ref[0, :, pl.ds(q0, LANES)][:, 0:1]

    blk = lax.broadcasted_iota(jnp.int32, (1, LANES), 1)
    skippable = ((c_q0 - cend_ref[0]) < thr_ref[...]) & (blk < qi)
    first = jnp.sum(skippable.astype(jnp.int32))

    def step(j, masked):
        k0 = pl.multiple_of(j * t, t)
        s = lax.dot_general(q_ref[...], k_ref[pl.ds(k0, t), :], (((1,), (1,)), ((), ())),
                            preferred_element_type=F32)
        s = s + (c_q0 - c_ref[0, :, pl.ds(k0, t)])
        if masked:
            row = lax.broadcasted_iota(jnp.int32, (t, t), 0)
            col = lax.broadcasted_iota(jnp.int32, (t, t), 1)
            s = jnp.where(row >= col, s, NEG_INF)
        m_prev = m_scr[...]
        m_next = jnp.maximum(m_prev, jnp.max(s, axis=1, keepdims=True))
        p = jnp.exp2(s - jnp.concatenate([m_next] * (t // LANES), axis=1))
        alpha = jnp.exp2(m_prev - m_next)
        l_scr[...] = alpha * l_scr[...] + jnp.sum(p, axis=1, keepdims=True)
        m_scr[...] = m_next
        acc_scr[...] = alpha * acc_scr[...] + jnp.dot(
            p.astype(BF16), v_ref[pl.ds(k0, t), :], preferred_element_type=F32)

    def body(j, carry):
        step(j, False)
        return carry

    lax.fori_loop(first, qi, body, 0)
    step(qi, True)
    o_ref[...] = (acc_scr[...] / l_scr[...]).astype(BF16)


def _fox_attn(qkv, c3, cend3, thr, *, t=512):
    return pl.pallas_call(
        functools.partial(_fox_attn_kernel, t=t),
        grid=(N_HEADS, SEQ // t),
        in_specs=[
            pl.BlockSpec((1, 1), lambda h, qi: (0, 0)),
            pl.BlockSpec((t, HEAD_DIM), lambda h, qi: (qi, h)),
            pl.BlockSpec((SEQ, HEAD_DIM), lambda h, qi: (0, N_HEADS + h)),
            pl.BlockSpec((SEQ, HEAD_DIM), lambda h, qi: (0, 2 * N_HEADS + h)),
            pl.BlockSpec((1, 1, SEQ), lambda h, qi: (h, 0, 0)),
            pl.BlockSpec((1, 1, LANES), lambda h, qi: (h, 0, 0)),
        ],
        out_specs=pl.BlockSpec((t, HEAD_DIM), lambda h, qi: (qi, h)),
        out_shape=jax.ShapeDtypeStruct((SEQ, D_MODEL), BF16),
        scratch_shapes=[
            pltpu.VMEM((t, LANES), F32),
            pltpu.VMEM((t, LANES), F32),
            pltpu.VMEM((t, HEAD_DIM), F32),
        ],
        compiler_params=_params(("arbitrary", "arbitrary")),
        name="fox_attn",
    )(thr, qkv, qkv, qkv, c3, cend3)


def _fox_out_kernel(x_ref, o_ref, w_ref, y_ref, wb_scr):
    @pl.when(pl.program_id(0) == 0)
    def _():
        wb_scr[...] = w_ref[...].astype(BF16)

    y_ref[...] = x_ref[...] + jnp.dot(o_ref[...], wb_scr[...],
                                      preferred_element_type=F32)


def _fox_out(x, o, w_out, layer, *, tm=512):
    return pl.pallas_call(
        _fox_out_kernel,
        grid=(SEQ // tm,),
        in_specs=[
            pl.BlockSpec((tm, D_MODEL), lambda i: (i, 0)),
            pl.BlockSpec((tm, D_MODEL), lambda i: (i, 0)),
            pl.BlockSpec((None, D_MODEL, D_MODEL), lambda i: (layer, 0, 0),
                         pipeline_mode=pl.Buffered(1)),
        ],
        out_specs=pl.BlockSpec((tm, D_MODEL), lambda i: (i, 0)),
        out_shape=jax.ShapeDtypeStruct((SEQ, D_MODEL), F32),
        scratch_shapes=[pltpu.VMEM((D_MODEL, D_MODEL), BF16)],
        compiler_params=_params(("arbitrary",)),
        name="fox_out",
    )(x, o, w_out)


def _pool_kernel(x_ref, g_ref, w_ref, b_ref, s_ref, y_ref, h_scr, *, tm):
    i = pl.program_id(0)

    @pl.when(i == 0)
    def _():
        h_scr[0:POOL_HALO, :] = jnp.zeros((POOL_HALO, D_MODEL), F32)

    x = x_ref[...]
    h_scr[POOL_HALO:POOL_HALO + tm, :] = _rmsnorm(x, g_ref[...])
    t = i * tm + lax.broadcasted_iota(jnp.int32, (tm, 1), 0)
    for gi, win in enumerate(POOL_WINDOWS):
        cols = slice(gi * POOL_GROUP, (gi + 1) * POOL_GROUP)
        h = h_scr[POOL_HALO:POOL_HALO + tm, cols]
        tot = h
        for d in range(1, win):
            tot = tot + h_scr[POOL_HALO - d:POOL_HALO - d + tm, cols]
        cnt = jnp.minimum(t + 1, win).astype(F32)
        y = (tot / cnt - h).astype(BF16)
        y = jnp.dot(y, w_ref[gi], preferred_element_type=F32)
        y_ref[:, cols] = x[:, cols] + (y + b_ref[:, cols]) * s_ref[:, cols]
    h_scr[0:POOL_HALO, :] = h_scr[tm:tm + POOL_HALO, :]


def _pool_mix(x, g, w, b, scale, *, tm=512):
    n_groups = len(POOL_WINDOWS)
    return pl.pallas_call(
        functools.partial(_pool_kernel, tm=tm),
        grid=(SEQ // tm,),
        in_specs=[
            pl.BlockSpec((tm, D_MODEL), lambda i: (i, 0)),
            pl.BlockSpec((1, D_MODEL), lambda i: (0, 0)),
            pl.BlockSpec((n_groups, POOL_GROUP, POOL_GROUP), lambda i: (0, 0, 0)),
            pl.BlockSpec((1, D_MODEL), lambda i: (0, 0)),
            pl.BlockSpec((1, D_MODEL), lambda i: (0, 0)),
        ],
        out_specs=pl.BlockSpec((tm, D_MODEL), lambda i: (i, 0)),
        out_shape=jax.ShapeDtypeStruct((SEQ, D_MODEL), F32),
        scratch_shapes=[pltpu.VMEM((POOL_HALO + tm, D_MODEL), F32)],
        compiler_params=_params(("arbitrary",)),
        name="pool_mix",
    )(x, g, w, b, scale)


def _ffn_kernel(*refs, convert_next):
    if convert_next:
        (x_ref, g_ref, wg_ref, wu_ref, wd_ref, ngu_ref, ndn_ref,
         y_ref, ngu_out, ndn_out, h_scr) = refs
        ngu_out[...] = ngu_ref[...].astype(BF16)
        ndn_out[...] = ndn_ref[...].astype(BF16)
    else:
        x_ref, g_ref, wg_ref, wu_ref, wd_ref, y_ref, h_scr = refs
    j = pl.program_id(1)

    @pl.when(j == 0)
    def _():
        x = x_ref[...]
        h_scr[...] = _rmsnorm(x, g_ref[...]).astype(BF16)
        y_ref[...] = x

    h = h_scr[...]
    gate = jnp.dot(h, wg_ref[...], preferred_element_type=F32)
    up = jnp.dot(h, wu_ref[...], preferred_element_type=F32)
    a = (gate * jax.nn.sigmoid(gate) * up).astype(BF16)
    y_ref[...] += jnp.dot(a, wd_ref[...], preferred_element_type=F32)


def _ffn(x, g, w_gu, w_down, next_w=None, *, tm=1024, tf=512):
    nf = D_FF // tf
    steps = (SEQ // tm) * nf
    in_specs = [
        pl.BlockSpec((tm, D_MODEL), lambda i, j: (i, 0)),
        pl.BlockSpec((1, D_MODEL), lambda i, j: (0, 0)),
        pl.BlockSpec((D_MODEL, tf), lambda i, j: (0, j)),
        pl.BlockSpec((D_MODEL, tf), lambda i, j: (0, nf + j)),
        pl.BlockSpec((tf, D_MODEL), lambda i, j: (j, 0)),
    ]
    out_specs = [pl.BlockSpec((tm, D_MODEL), lambda i, j: (i, 0))]
    out_shape = [jax.ShapeDtypeStruct((SEQ, D_MODEL), F32)]
    args = [x, g, w_gu, w_gu, w_down]
    if next_w is not None:
        n_gu, n_down, layer = next_w
        gu_cols = 2 * D_FF // steps
        dn_rows = D_FF // steps
        assert gu_cols % LANES == 0 and dn_rows % 16 == 0
        in_specs += [
            pl.BlockSpec((None, D_MODEL, gu_cols), lambda i, j: (layer, 0, i * nf + j)),
            pl.BlockSpec((None, dn_rows, D_MODEL), lambda i, j: (layer, i * nf + j, 0)),
        ]
        out_specs += [
            pl.BlockSpec((D_MODEL, gu_cols), lambda i, j: (0, i * nf + j)),
            pl.BlockSpec((dn_rows, D_MODEL), lambda i, j: (i * nf + j, 0)),
        ]
        out_shape += [jax.ShapeDtypeStruct((D_MODEL, 2 * D_FF), BF16),
                      jax.ShapeDtypeStruct((D_FF, D_MODEL), BF16)]
        args += [n_gu, n_down]
    return pl.pallas_call(
        functools.partial(_ffn_kernel, convert_next=next_w is not None),
        grid=(SEQ // tm, nf),
        in_specs=in_specs,
        out_specs=out_specs,
        out_shape=out_shape,
        scratch_shapes=[pltpu.VMEM((tm, D_MODEL), BF16)],
        compiler_params=_params(("arbitrary", "arbitrary")),
        name="ffn",
    )(*args)


def kernel(x, mix_norm_g, ffn_norm_g, fox_w_in, fox_b_f, fox_q_norm_g, fox_k_norm_g,
           fox_w_out, pool_w, pool_b, pool_scale, ffn_w_gate_up, ffn_w_down):
    assert x.shape == (1, SEQ, D_MODEL)
    x = x.reshape(SEQ, D_MODEL)
    tri = jnp.tril(jnp.ones((LANES, LANES), BF16))
    q_scale = HEAD_DIM ** -0.5 * LOG2E
    w_gu = ffn_w_gate_up[0].astype(BF16)
    w_down = ffn_w_down[0].astype(BF16)
    for i in range(DEPTH):
        j = i // 2
        g_mix = mix_norm_g[i][None, :]
        if i % 2 == 0:
            wf = jnp.pad(fox_w_in[j, :, 3 * D_MODEL:], ((0, 0), (0, LANES - N_HEADS)))
            bf = jnp.pad(fox_b_f[j], (0, LANES - N_HEADS))[None, :]
            gains = jnp.concatenate([
                jnp.tile(fox_q_norm_g[j] * q_scale, N_HEADS),
                jnp.tile(fox_k_norm_g[j], N_HEADS),
                jnp.ones((D_MODEL,), F32)])[None, :]
            qkv, ct = _fox_proj(x, g_mix, fox_w_in, j, gains, wf, bf, tri)
            qk_bound = (1.02 * HEAD_DIM * q_scale * jnp.max(jnp.abs(fox_q_norm_g[j]))
                        * jnp.max(jnp.abs(fox_k_norm_g[j])))
            thr = jnp.reshape(-(SKIP_EXP2 + 2.0 * qk_bound), (1, 1)).astype(F32)
            cend = jnp.pad(ct[:, ATTN_BLOCK - 1::ATTN_BLOCK],
                           ((0, 0), (0, LANES - SEQ // ATTN_BLOCK)))
            o = _fox_attn(qkv, ct[:, None, :], cend[:, None, :], thr, t=ATTN_BLOCK)
            x = _fox_out(x, o, fox_w_out, j)
        else:
            x = _pool_mix(x, g_mix, pool_w[j].astype(BF16), pool_b[j][None, :],
                          pool_scale[j][None, :])
        if i + 1 < DEPTH:
            x, w_gu, w_down = _ffn(x, ffn_norm_g[i][None, :], w_gu, w_down,
                                   (ffn_w_gate_up, ffn_w_down, i + 1))
        else:
            x, = _ffn(x, ffn_norm_g[i][None, :], w_gu, w_down)
    return x.reshape(1, SEQ, D_MODEL)
```

```python
import functools
import math

import jax
import jax.numpy as jnp
from jax import lax
from jax.experimental import pallas as pl
from jax.experimental.pallas import tpu as pltpu

D_MODEL = 2048
SEQ = 8192
DEPTH = 4
HEAD_DIM = 128
N_HEADS = D_MODEL // HEAD_DIM
POOL_WINDOWS = (2, 4, 8, 16)
POOL_GROUP = D_MODEL // len(POOL_WINDOWS)
POOL_HALO = max(POOL_WINDOWS)
D_FF = 5632
RMS_EPS = 1e-6
NEG_INF = -1e30
LOG2E = math.log2(math.e)

LANES = 128
VMEM_LIMIT = 62 * 1024 * 1024

F32 = jnp.float32
BF16 = jnp.bfloat16


def _params(semantics):
    return pltpu.CompilerParams(dimension_semantics=semantics,
                                vmem_limit_bytes=VMEM_LIMIT)


def _rmsnorm(x, g):
    return x * lax.rsqrt(jnp.mean(x * x, axis=-1, keepdims=True) + RMS_EPS) * g


def _split3_bf16(x):
    x1 = x.astype(BF16)
    r1 = x - x1.astype(F32)
    x2 = r1.astype(BF16)
    x3 = (r1 - x2.astype(F32)).astype(BF16)
    return x1, x2, x3


ROW_CHUNK = 256
_NT_DIMS = (((1,), (1,)), ((), ()))


def _fox_proj_kernel(x_ref, g_ref, w_ref, gain_ref, wf_ref, bf_ref, tri_ref,
                     qkv_ref, ct_ref, h_scr, carry_scr, *, tm, tn):
    i = pl.program_id(0)
    j = pl.program_id(1)

    @pl.when(j == 0)
    def _():
        hb = _rmsnorm(x_ref[...], g_ref[...]).astype(BF16)
        h_scr[...] = hb
        f = lax.dot_general(hb, wf_ref[...].astype(BF16), _NT_DIMS,
                            preferred_element_type=F32) + bf_ref[...]
        logf = jnp.minimum(f, 0.0) - jnp.log1p(jnp.exp(-jnp.abs(f)))

        @pl.when(i == 0)
        def _():
            carry_scr[...] = jnp.zeros_like(carry_scr)

        tri = tri_ref[...]
        carry = carry_scr[0:1, :]
        chunks = []
        for r in range(tm // LANES):
            x1, x2, x3 = _split3_bf16(logf[r * LANES:(r + 1) * LANES, :])
            cs = (jnp.dot(tri, x1, preferred_element_type=F32)
                  + jnp.dot(tri, x2, preferred_element_type=F32)
                  + jnp.dot(tri, x3, preferred_element_type=F32)) + carry
            carry = cs[LANES - 1:LANES, :]
            chunks.append(cs)
        carry_scr[0:1, :] = carry
        c = jnp.concatenate(chunks, axis=0) * LOG2E
        ct_ref[...] = c.T[:N_HEADS, :]

    w_t = w_ref[...].astype(BF16)
    is_qk = (j < 2 * D_MODEL // tn).astype(F32)
    for rc in range(tm // ROW_CHUNK):
        rows = slice(rc * ROW_CHUNK, (rc + 1) * ROW_CHUNK)
        r = lax.dot_general(h_scr[rows, :], w_t, _NT_DIMS,
                            preferred_element_type=F32)
        for hh in range(tn // HEAD_DIM):
            sl = slice(hh * HEAD_DIM, (hh + 1) * HEAD_DIM)
            rh = r[:, sl]
            inv = lax.rsqrt(jnp.mean(rh * rh, axis=-1, keepdims=True) + RMS_EPS)
            scale = is_qk * inv + (1.0 - is_qk)
            qkv_ref[rows, sl] = (rh * scale * gain_ref[:, sl]).astype(BF16)


def _fox_proj(x, g, w_in_t, layer, gains, wf_t, bf, tri, *, tm=1024, tn=1024):
    n = 3 * D_MODEL
    return pl.pallas_call(
        functools.partial(_fox_proj_kernel, tm=tm, tn=tn),
        grid=(SEQ // tm, n // tn),
        in_specs=[
            pl.BlockSpec((tm, D_MODEL), lambda i, j: (i, 0)),
            pl.BlockSpec((1, D_MODEL), lambda i, j: (0, 0)),
            pl.BlockSpec((None, tn, D_MODEL), lambda i, j: (layer, j, 0)),
            pl.BlockSpec((1, tn), lambda i, j: (0, j)),
            pl.BlockSpec((LANES, D_MODEL), lambda i, j: (0, 0)),
            pl.BlockSpec((1, LANES), lambda i, j: (0, 0)),
            pl.BlockSpec((LANES, LANES), lambda i, j: (0, 0)),
        ],
        out_specs=[
            pl.BlockSpec((tm, tn), lambda i, j: (i, j)),
            pl.BlockSpec((N_HEADS, tm), lambda i, j: (0, i)),
        ],
        out_shape=[
            jax.ShapeDtypeStruct((SEQ, n), BF16),
            jax.ShapeDtypeStruct((N_HEADS, SEQ), F32),
        ],
        scratch_shapes=[
            pltpu.VMEM((tm, D_MODEL), BF16),
            pltpu.VMEM((8, LANES), F32),
        ],
        compiler_params=_params(("arbitrary", "arbitrary")),
        name="fox_proj",
    )(x, g, w_in_t, gains, wf_t, bf, tri)


SKIP_EXP2 = 160.0
ATTN_BLOCK = 512


def _fox_attn_kernel(*refs, t, convert):
    if convert:
        (thr_ref, q_ref, k_ref, v_ref, c_ref, cend_ref, wa_ref, wb_ref,
         o_ref, wa_out, wb_out, m_scr, l_scr, acc_scr) = refs
        wa_out[...] = wa_ref[...].astype(BF16)
        wb_out[...] = wb_ref[...].astype(BF16)
    else:
        (thr_ref, q_ref, k_ref, v_ref, c_ref, cend_ref,
         o_ref, m_scr, l_scr, acc_scr) = refs
    blk = lax.broadcasted_iota(jnp.int32, (1, LANES), 1)
    cend = cend_ref[0]
    thr = thr_ref[...]

    def q_block(qi, carry):
        q0 = pl.multiple_of(qi * t, t)
        q = q_ref[pl.ds(q0, t), :]
        m_scr[...] = jnp.full_like(m_scr, NEG_INF)
        l_scr[...] = jnp.zeros_like(l_scr)
        acc_scr[...] = jnp.zeros_like(acc_scr)
        c_q0 = c_ref[0, :, pl.ds(q0, LANES)][:, 0:1]
        skippable = ((c_q0 - cend) < thr) & (blk < qi)
        first = jnp.sum(skippable.astype(jnp.int32))

        def step(j, masked):
            k0 = pl.multiple_of(j * t, t)
            s = lax.dot_general(q, k_ref[pl.ds(k0, t), :], (((1,), (1,)), ((), ())),
                                preferred_element_type=F32)
            s = s + (c_q0 - c_ref[0, :, pl.ds(k0, t)])
            if masked:
                row = lax.broadcasted_iota(jnp.int32, (t, t), 0)
                col = lax.broadcasted_iota(jnp.int32, (t, t), 1)
                s = jnp.where(row >= col, s, NEG_INF)
            m_prev = m_scr[...]
            m_next = jnp.maximum(m_prev, jnp.max(s, axis=1, keepdims=True))
            p = jnp.exp2(s - jnp.concatenate([m_next] * (t // LANES), axis=1))
            alpha = jnp.exp2(m_prev - m_next)
            l_scr[...] = alpha * l_scr[...] + jnp.sum(p, axis=1, keepdims=True)
            m_scr[...] = m_next
            acc_scr[...] = alpha * acc_scr[...] + jnp.dot(
                p.astype(BF16), v_ref[pl.ds(k0, t), :], preferred_element_type=F32)

        def kv_block(j, c):
            step(j, False)
            return c

        lax.fori_loop(first, qi, kv_block, 0)
        step(qi, True)
        o_ref[pl.ds(q0, t), :] = (acc_scr[...] / l_scr[...]).astype(BF16)
        return carry

    lax.fori_loop(0, SEQ // t, q_block, 0)


def _fox_attn(qkv, c3, cend3, thr, cast_w=None, *, t=512):
    in_specs = [
        pl.BlockSpec((1, 1), lambda h: (0, 0)),
        pl.BlockSpec((SEQ, HEAD_DIM), lambda h: (0, h)),
        pl.BlockSpec((SEQ, HEAD_DIM), lambda h: (0, N_HEADS + h)),
        pl.BlockSpec((SEQ, HEAD_DIM), lambda h: (0, 2 * N_HEADS + h)),
        pl.BlockSpec((1, 1, SEQ), lambda h: (h, 0, 0)),
        pl.BlockSpec((1, 1, LANES), lambda h: (h, 0, 0)),
    ]
    out_specs = [pl.BlockSpec((SEQ, HEAD_DIM), lambda h: (0, h))]
    out_shape = [jax.ShapeDtypeStruct((SEQ, D_MODEL), BF16)]
    args = [thr, qkv, qkv, qkv, c3, cend3]
    if cast_w is not None:
        wa, wb, layer = cast_w
        for w in (wa, wb):
            rows, cols = w.shape[1] // N_HEADS, w.shape[2]
            assert rows % 16 == 0
            in_specs.append(pl.BlockSpec((None, rows, cols), lambda h: (layer, h, 0)))
            out_specs.append(pl.BlockSpec((rows, cols), lambda h: (h, 0)))
            out_shape.append(jax.ShapeDtypeStruct(w.shape[1:], BF16))
        args += [wa, wb]
    return pl.pallas_call(
        functools.partial(_fox_attn_kernel, t=t, convert=cast_w is not None),
        grid=(N_HEADS,),
        in_specs=in_specs,
        out_specs=out_specs,
        out_shape=out_shape,
        scratch_shapes=[
            pltpu.VMEM((t, LANES), F32),
            pltpu.VMEM((t, LANES), F32),
            pltpu.VMEM((t, HEAD_DIM), F32),
        ],
        compiler_params=_params(("arbitrary",)),
        name="fox_attn",
    )(*args)


def _fox_out_kernel(x_ref, o_ref, w_ref, y_ref, wb_scr):
    @pl.when(pl.program_id(0) == 0)
    def _():
        wb_scr[...] = w_ref[...].astype(BF16)

    y_ref[...] = x_ref[...] + jnp.dot(o_ref[...], wb_scr[...],
                                      preferred_element_type=F32)


def _fox_out(x, o, w_out, layer, *, tm=512):
    return pl.pallas_call(
        _fox_out_kernel,
        grid=(SEQ // tm,),
        in_specs=[
            pl.BlockSpec((tm, D_MODEL), lambda i: (i, 0)),
            pl.BlockSpec((tm, D_MODEL), lambda i: (i, 0)),
            pl.BlockSpec((None, D_MODEL, D_MODEL), lambda i: (layer, 0, 0),
                         pipeline_mode=pl.Buffered(1)),
        ],
        out_specs=pl.BlockSpec((tm, D_MODEL), lambda i: (i, 0)),
        out_shape=jax.ShapeDtypeStruct((SEQ, D_MODEL), F32),
        scratch_shapes=[pltpu.VMEM((D_MODEL, D_MODEL), BF16)],
        compiler_params=_params(("arbitrary",)),
        name="fox_out",
    )(x, o, w_out)


def _pool_kernel(x_ref, g_ref, w_ref, b_ref, s_ref, y_ref, h_scr, *, tm):
    i = pl.program_id(0)

    @pl.when(i == 0)
    def _():
        h_scr[0:POOL_HALO, :] = jnp.zeros((POOL_HALO, D_MODEL), F32)

    x = x_ref[...]
    h_scr[POOL_HALO:POOL_HALO + tm, :] = _rmsnorm(x, g_ref[...])
    t = i * tm + lax.broadcasted_iota(jnp.int32, (tm, 1), 0)
    for gi, win in enumerate(POOL_WINDOWS):
        cols = slice(gi * POOL_GROUP, (gi + 1) * POOL_GROUP)
        h = h_scr[POOL_HALO:POOL_HALO + tm, cols]
        tot = h
        for d in range(1, win):
            tot = tot + h_scr[POOL_HALO - d:POOL_HALO - d + tm, cols]
        cnt = jnp.minimum(t + 1, win).astype(F32)
        y = (tot / cnt - h).astype(BF16)
        y = jnp.dot(y, w_ref[gi], preferred_element_type=F32)
        y_ref[:, cols] = x[:, cols] + (y + b_ref[:, cols]) * s_ref[:, cols]
    h_scr[0:POOL_HALO, :] = h_scr[tm:tm + POOL_HALO, :]


def _pool_mix(x, g, w, b, scale, *, tm=512):
    n_groups = len(POOL_WINDOWS)
    return pl.pallas_call(
        functools.partial(_pool_kernel, tm=tm),
        grid=(SEQ // tm,),
        in_specs=[
            pl.BlockSpec((tm, D_MODEL), lambda i: (i, 0)),
            pl.BlockSpec((1, D_MODEL), lambda i: (0, 0)),
            pl.BlockSpec((n_groups, POOL_GROUP, POOL_GROUP), lambda i: (0, 0, 0)),
            pl.BlockSpec((1, D_MODEL), lambda i: (0, 0)),
            pl.BlockSpec((1, D_MODEL), lambda i: (0, 0)),
        ],
        out_specs=pl.BlockSpec((tm, D_MODEL), lambda i: (i, 0)),
        out_shape=jax.ShapeDtypeStruct((SEQ, D_MODEL), F32),
        scratch_shapes=[pltpu.VMEM((POOL_HALO + tm, D_MODEL), F32)],
        compiler_params=_params(("arbitrary",)),
        name="pool_mix",
    )(x, g, w, b, scale)


def _ffn_kernel(*refs, convert_next):
    if convert_next:
        (x_ref, g_ref, wg_ref, wu_ref, wd_ref, ngu_ref, ndn_ref,
         y_ref, ngu_out, ndn_out, h_scr) = refs
        ngu_out[...] = ngu_ref[...].astype(BF16)
        ndn_out[...] = ndn_ref[...].astype(BF16)
    else:
        x_ref, g_ref, wg_ref, wu_ref, wd_ref, y_ref, h_scr = refs
    j = pl.program_id(1)

    @pl.when(j == 0)
    def _():
        x = x_ref[...]
        h_scr[...] = _rmsnorm(x, g_ref[...]).astype(BF16)
        y_ref[...] = x

    h = h_scr[...]
    gate = jnp.dot(h, wg_ref[...], preferred_element_type=F32)
    up = jnp.dot(h, wu_ref[...], preferred_element_type=F32)
    a = (gate * jax.nn.sigmoid(gate) * up).astype(BF16)
    y_ref[...] += jnp.dot(a, wd_ref[...], preferred_element_type=F32)


def _ffn(x, g, w_gu, w_down, next_w=None, *, tm=1024, tf=512):
    nf = D_FF // tf
    steps = (SEQ // tm) * nf
    in_specs = [
        pl.BlockSpec((tm, D_MODEL), lambda i, j: (i, 0)),
        pl.BlockSpec((1, D_MODEL), lambda i, j: (0, 0)),
        pl.BlockSpec((D_MODEL, tf), lambda i, j: (0, j)),
        pl.BlockSpec((D_MODEL, tf), lambda i, j: (0, nf + j)),
        pl.BlockSpec((tf, D_MODEL), lambda i, j: (j, 0)),
    ]
    out_specs = [pl.BlockSpec((tm, D_MODEL), lambda i, j: (i, 0))]
    out_shape = [jax.ShapeDtypeStruct((SEQ, D_MODEL), F32)]
    args = [x, g, w_gu, w_gu, w_down]
    if next_w is not None:
        n_gu, n_down, layer = next_w
        gu_cols = 2 * D_FF // steps
        dn_rows = D_FF // steps
        assert gu_cols % LANES == 0 and dn_rows % 16 == 0
        in_specs += [
            pl.BlockSpec((None, D_MODEL, gu_cols), lambda i, j: (layer, 0, i * nf + j)),
            pl.BlockSpec((None, dn_rows, D_MODEL), lambda i, j: (layer, i * nf + j, 0)),
        ]
        out_specs += [
            pl.BlockSpec((D_MODEL, gu_cols), lambda i, j: (0, i * nf + j)),
            pl.BlockSpec((dn_rows, D_MODEL), lambda i, j: (i * nf + j, 0)),
        ]
        out_shape += [jax.ShapeDtypeStruct((D_MODEL, 2 * D_FF), BF16),
                      jax.ShapeDtypeStruct((D_FF, D_MODEL), BF16)]
        args += [n_gu, n_down]
    return pl.pallas_call(
        functools.partial(_ffn_kernel, convert_next=next_w is not None),
        grid=(SEQ // tm, nf),
        in_specs=in_specs,
        out_specs=out_specs,
        out_shape=out_shape,
        scratch_shapes=[pltpu.VMEM((tm, D_MODEL), BF16)],
        compiler_params=_params(("arbitrary", "arbitrary")),
        name="ffn",
    )(*args)


def kernel(x, mix_norm_g, ffn_norm_g, fox_w_in, fox_b_f, fox_q_norm_g, fox_k_norm_g,
           fox_w_out, pool_w, pool_b, pool_scale, ffn_w_gate_up, ffn_w_down):
    assert x.shape == (1, SEQ, D_MODEL)
    x = x.reshape(SEQ, D_MODEL)
    tri = jnp.tril(jnp.ones((LANES, LANES), BF16))
    q_scale = HEAD_DIM ** -0.5 * LOG2E
    fox_w_in_t = jnp.swapaxes(fox_w_in, 1, 2)
    w_gu = w_down = None
    for i in range(DEPTH):
        j = i // 2
        g_mix = mix_norm_g[i][None, :]
        if i % 2 == 0:
            wf_t = jnp.pad(fox_w_in_t[j, 3 * D_MODEL:, :], ((0, LANES - N_HEADS), (0, 0)))
            bf = jnp.pad(fox_b_f[j], (0, LANES - N_HEADS))[None, :]
            gains = jnp.concatenate([
                jnp.tile(fox_q_norm_g[j] * q_scale, N_HEADS),
                jnp.tile(fox_k_norm_g[j], N_HEADS),
                jnp.ones((D_MODEL,), F32)])[None, :]
            qkv, ct = _fox_proj(x, g_mix, fox_w_in_t, j, gains, wf_t, bf, tri)
            qk_bound = (1.02 * HEAD_DIM * q_scale * jnp.max(jnp.abs(fox_q_norm_g[j]))
                        * jnp.max(jnp.abs(fox_k_norm_g[j])))
            thr = jnp.reshape(-(SKIP_EXP2 + 2.0 * qk_bound), (1, 1)).astype(F32)
            cend = jnp.pad(ct[:, ATTN_BLOCK - 1::ATTN_BLOCK],
                           ((0, 0), (0, LANES - SEQ // ATTN_BLOCK)))
            if i == 0:
                o, w_gu, w_down = _fox_attn(qkv, ct[:, None, :], cend[:, None, :], thr,
                                            (ffn_w_gate_up, ffn_w_down, 0), t=ATTN_BLOCK)
            else:
                o, = _fox_attn(qkv, ct[:, None, :], cend[:, None, :], thr, t=ATTN_BLOCK)
            x = _fox_out(x, o, fox_w_out, j)
        else:
            x = _pool_mix(x, g_mix, pool_w[j].astype(BF16), pool_b[j][None, :],
                          pool_scale[j][None, :])
        if i + 1 < DEPTH:
            x, w_gu, w_down = _ffn(x, ffn_norm_g[i][None, :], w_gu, w_down,
                                   (ffn_w_gate_up, ffn_w_down, i + 1))
        else:
            x, = _ffn(x, ffn_norm_g[i][None, :], w_gu, w_down)
    return x.reshape(1, SEQ, D_MODEL)
```

```python
import functools
import math

import jax
import jax.numpy as jnp
from jax import lax
from jax.experimental import pallas as pl
from jax.experimental.pallas import tpu as pltpu

D_MODEL = 2048
SEQ = 8192
DEPTH = 4
HEAD_DIM = 128
N_HEADS = D_MODEL // HEAD_DIM
POOL_WINDOWS = (2, 4, 8, 16)
POOL_GROUP = D_MODEL // len(POOL_WINDOWS)
POOL_HALO = 32
D_FF = 5632
RMS_EPS = 1e-6
NEG_INF = -1e30
LOG2E = math.log2(math.e)

LANES = 128
VMEM_LIMIT = 62 * 1024 * 1024

F32 = jnp.float32
BF16 = jnp.bfloat16


def _params(semantics):
    return pltpu.CompilerParams(dimension_semantics=semantics,
                                vmem_limit_bytes=VMEM_LIMIT)


def _rmsnorm(x, g):
    return x * lax.rsqrt(jnp.mean(x * x, axis=-1, keepdims=True) + RMS_EPS) * g


def _split3_bf16(x):
    x1 = x.astype(BF16)
    r1 = x - x1.astype(F32)
    x2 = r1.astype(BF16)
    x3 = (r1 - x2.astype(F32)).astype(BF16)
    return x1, x2, x3


ROW_CHUNK = 256
_NT_DIMS = (((1,), (1,)), ((), ()))


def _fox_proj_kernel(x_ref, g_ref, w_ref, gain_ref, wf_ref, bf_ref, tri_ref,
                     qkv_ref, ct_ref, h_scr, carry_scr, *, tm, tn):
    i = pl.program_id(0)
    j = pl.program_id(1)

    @pl.when(j == 0)
    def _():
        hb = _rmsnorm(x_ref[...], g_ref[...]).astype(BF16)
        h_scr[...] = hb
        f = lax.dot_general(hb, wf_ref[...].astype(BF16), _NT_DIMS,
                            preferred_element_type=F32) + bf_ref[...]
        logf = jnp.minimum(f, 0.0) - jnp.log1p(jnp.exp(-jnp.abs(f)))

        @pl.when(i == 0)
        def _():
            carry_scr[...] = jnp.zeros_like(carry_scr)

        tri = tri_ref[...]
        carry = carry_scr[0:1, :]
        chunks = []
        for r in range(tm // LANES):
            x1, x2, x3 = _split3_bf16(logf[r * LANES:(r + 1) * LANES, :])
            cs = (jnp.dot(tri, x1, preferred_element_type=F32)
                  + jnp.dot(tri, x2, preferred_element_type=F32)
                  + jnp.dot(tri, x3, preferred_element_type=F32)) + carry
            carry = cs[LANES - 1:LANES, :]
            chunks.append(cs)
        carry_scr[0:1, :] = carry
        c = jnp.concatenate(chunks, axis=0) * LOG2E
        ct_ref[...] = c.T[:N_HEADS, :]

    w_t = w_ref[...].astype(BF16)
    is_qk = (j < 2 * D_MODEL // tn).astype(F32)
    for rc in range(tm // ROW_CHUNK):
        rows = slice(rc * ROW_CHUNK, (rc + 1) * ROW_CHUNK)
        r = lax.dot_general(h_scr[rows, :], w_t, _NT_DIMS,
                            preferred_element_type=F32)
        for hh in range(tn // HEAD_DIM):
            sl = slice(hh * HEAD_DIM, (hh + 1) * HEAD_DIM)
            rh = r[:, sl]
            inv = lax.rsqrt(jnp.mean(rh * rh, axis=-1, keepdims=True) + RMS_EPS)
            scale = is_qk * inv + (1.0 - is_qk)
            qkv_ref[rows, sl] = (rh * scale * gain_ref[:, sl]).astype(BF16)


def _fox_proj(x, g, w_in_t, layer, gains, wf_t, bf, tri, *, tm=1024, tn=1024):
    n = 3 * D_MODEL
    return pl.pallas_call(
        functools.partial(_fox_proj_kernel, tm=tm, tn=tn),
        grid=(SEQ // tm, n // tn),
        in_specs=[
            pl.BlockSpec((tm, D_MODEL), lambda i, j: (i, 0)),
            pl.BlockSpec((1, D_MODEL), lambda i, j: (0, 0)),
            pl.BlockSpec((None, tn, D_MODEL), lambda i, j: (layer, j, 0)),
            pl.BlockSpec((1, tn), lambda i, j: (0, j)),
            pl.BlockSpec((LANES, D_MODEL), lambda i, j: (0, 0)),
            pl.BlockSpec((1, LANES), lambda i, j: (0, 0)),
            pl.BlockSpec((LANES, LANES), lambda i, j: (0, 0)),
        ],
        out_specs=[
            pl.BlockSpec((tm, tn), lambda i, j: (i, j)),
            pl.BlockSpec((N_HEADS, tm), lambda i, j: (0, i)),
        ],
        out_shape=[
            jax.ShapeDtypeStruct((SEQ, n), BF16),
            jax.ShapeDtypeStruct((N_HEADS, SEQ), F32),
        ],
        scratch_shapes=[
            pltpu.VMEM((tm, D_MODEL), BF16),
            pltpu.VMEM((8, LANES), F32),
        ],
        compiler_params=_params(("arbitrary", "arbitrary")),
        name="fox_proj",
    )(x, g, w_in_t, gains, wf_t, bf, tri)


SKIP_EXP2 = 160.0
ATTN_BLOCK = 512
MAX_FIXED_SHIFT_BOUND = 48.0


def _fox_attn_kernel(*refs, t, convert):
    if convert:
        (par_ref, q_ref, k_ref, v_ref, c_ref, cend_ref, wa_ref, wb_ref,
         o_ref, wa_out, wb_out, m_scr, l_scr, acc_scr) = refs
        wa_out[...] = wa_ref[...].astype(BF16)
        wb_out[...] = wb_ref[...].astype(BF16)
    else:
        (par_ref, q_ref, k_ref, v_ref, c_ref, cend_ref,
         o_ref, m_scr, l_scr, acc_scr) = refs
    blk = lax.broadcasted_iota(jnp.int32, (1, LANES), 1)
    cend = cend_ref[0]
    thr = par_ref[0]
    qk_bound = par_ref[1]

    def causal(x):
        row = lax.broadcasted_iota(jnp.int32, (t, t), 0)
        col = lax.broadcasted_iota(jnp.int32, (t, t), 1)
        return jnp.where(row >= col, x, NEG_INF)

    def scores(q, c_q0, j):
        k0 = pl.multiple_of(j * t, t)
        s = lax.dot_general(q, k_ref[pl.ds(k0, t), :], _NT_DIMS,
                            preferred_element_type=F32)
        return s + (c_q0 - c_ref[0, :, pl.ds(k0, t)])

    def v_block(j):
        return v_ref[pl.ds(pl.multiple_of(j * t, t), t), :]

    def q_block(qi, fixed_shift):
        q0 = pl.multiple_of(qi * t, t)
        q = q_ref[pl.ds(q0, t), :]
        c_q = c_ref[0, :, pl.ds(q0, t)]
        c_q0 = c_q[:, 0:1]
        skippable = ((c_q0 - cend) < thr) & (blk < qi)
        first = jnp.sum(skippable.astype(jnp.int32))
        l_scr[...] = jnp.zeros_like(l_scr)
        acc_scr[...] = jnp.zeros_like(acc_scr)

        if fixed_shift:
            shift = qk_bound + (c_q0 - c_q)
            shift = jnp.broadcast_to(shift, (LANES, t)).T
            shift = jnp.concatenate([shift] * (t // LANES), axis=1)

            def step(j, masked):
                x = scores(q, c_q0, j) - shift
                p = jnp.exp2(causal(x) if masked else x)
                part = p[:, 0:LANES]
                for n in range(1, t // LANES):
                    part = part + p[:, n * LANES:(n + 1) * LANES]
                l_scr[...] += part
                acc_scr[...] += jnp.dot(p.astype(BF16), v_block(j),
                                        preferred_element_type=F32)
        else:
            m_scr[...] = jnp.full_like(m_scr, NEG_INF)

            def step(j, masked):
                s = scores(q, c_q0, j)
                if masked:
                    s = causal(s)
                m_prev = m_scr[...]
                m_next = jnp.maximum(m_prev, jnp.max(s, axis=1, keepdims=True))
                p = jnp.exp2(s - jnp.concatenate([m_next] * (t // LANES), axis=1))
                alpha = jnp.exp2(m_prev - m_next)
                l_scr[...] = alpha * l_scr[...] + jnp.sum(p, axis=1, keepdims=True)
                m_scr[...] = m_next
                acc_scr[...] = alpha * acc_scr[...] + jnp.dot(
                    p.astype(BF16), v_block(j), preferred_element_type=F32)

        n_pairs = (qi - first) // 2

        def kv_pair(n, c):
            j = first + 2 * n
            step(j, False)
            step(j + 1, False)
            return c

        lax.fori_loop(0, n_pairs, kv_pair, 0)

        @pl.when(first + 2 * n_pairs < qi)
        def _():
            step(qi - 1, False)

        step(qi, True)
        l = l_scr[...]
        if fixed_shift:
            l = jnp.sum(l, axis=1, keepdims=True)
        o_ref[pl.ds(q0, t), :] = (acc_scr[...] / l).astype(BF16)

    def all_q_blocks(fixed_shift):
        def body(qi, carry):
            q_block(qi, fixed_shift)
            return carry
        lax.fori_loop(0, SEQ // t, body, 0)

    @pl.when(qk_bound < MAX_FIXED_SHIFT_BOUND)
    def _():
        all_q_blocks(True)

    @pl.when(qk_bound >= MAX_FIXED_SHIFT_BOUND)
    def _():
        all_q_blocks(False)


def _fox_attn(qkv, c3, cend3, par, cast_w=None, *, t=512):
    in_specs = [
        pl.BlockSpec(memory_space=pltpu.SMEM),
        pl.BlockSpec((SEQ, HEAD_DIM), lambda h: (0, h)),
        pl.BlockSpec((SEQ, HEAD_DIM), lambda h: (0, N_HEADS + h)),
        pl.BlockSpec((SEQ, HEAD_DIM), lambda h: (0, 2 * N_HEADS + h)),
        pl.BlockSpec((1, 1, SEQ), lambda h: (h, 0, 0)),
        pl.BlockSpec((1, 1, LANES), lambda h: (h, 0, 0)),
    ]
    out_specs = [pl.BlockSpec((SEQ, HEAD_DIM), lambda h: (0, h))]
    out_shape = [jax.ShapeDtypeStruct((SEQ, D_MODEL), BF16)]
    args = [par, qkv, qkv, qkv, c3, cend3]
    if cast_w is not None:
        wa, wb, layer = cast_w
        for w in (wa, wb):
            rows, cols = w.shape[1] // N_HEADS, w.shape[2]
            assert rows % 16 == 0
            in_specs.append(pl.BlockSpec((None, rows, cols), lambda h: (layer, h, 0)))
            out_specs.append(pl.BlockSpec((rows, cols), lambda h: (h, 0)))
            out_shape.append(jax.ShapeDtypeStruct(w.shape[1:], BF16))
        args += [wa, wb]
    return pl.pallas_call(
        functools.partial(_fox_attn_kernel, t=t, convert=cast_w is not None),
        grid=(N_HEADS,),
        in_specs=in_specs,
        out_specs=out_specs,
        out_shape=out_shape,
        scratch_shapes=[
            pltpu.VMEM((t, LANES), F32),
            pltpu.VMEM((t, LANES), F32),
            pltpu.VMEM((t, HEAD_DIM), F32),
        ],
        compiler_params=_params(("arbitrary",)),
        name="fox_attn",
    )(*args)


def _fox_out_kernel(x_ref, o_ref, w_ref, y_ref, wb_scr):
    @pl.when(pl.program_id(0) == 0)
    def _():
        wb_scr[...] = w_ref[...].astype(BF16)

    y_ref[...] = x_ref[...] + jnp.dot(o_ref[...], wb_scr[...],
                                      preferred_element_type=F32)


def _fox_out(x, o, w_out, layer, *, tm=512):
    return pl.pallas_call(
        _fox_out_kernel,
        grid=(SEQ // tm,),
        in_specs=[
            pl.BlockSpec((tm, D_MODEL), lambda i: (i, 0)),
            pl.BlockSpec((tm, D_MODEL), lambda i: (i, 0)),
            pl.BlockSpec((None, D_MODEL, D_MODEL), lambda i: (layer, 0, 0),
                         pipeline_mode=pl.Buffered(1)),
        ],
        out_specs=pl.BlockSpec((tm, D_MODEL), lambda i: (i, 0)),
        out_shape=jax.ShapeDtypeStruct((SEQ, D_MODEL), F32),
        scratch_shapes=[pltpu.VMEM((D_MODEL, D_MODEL), BF16)],
        compiler_params=_params(("arbitrary",)),
        name="fox_out",
    )(x, o, w_out)


def _pool_kernel(x_ref, g_ref, w_ref, b_ref, s_ref, y_ref, h_scr, a_scr, b_scr, *, tm):
    i = pl.program_id(0)
    end = POOL_HALO + tm

    @pl.when(i == 0)
    def _():
        h_scr[0:POOL_HALO, :] = jnp.zeros((POOL_HALO, D_MODEL), F32)

    x = x_ref[...]
    h_scr[POOL_HALO:end, :] = _rmsnorm(x, g_ref[...])
    t = i * tm + lax.broadcasted_iota(jnp.int32, (tm, 1), 0)
    for gi, win in enumerate(POOL_WINDOWS):
        cols = slice(gi * POOL_GROUP, (gi + 1) * POOL_GROUP)
        h = h_scr[POOL_HALO:end, cols]
        levels = win.bit_length() - 1
        src = lambda lo, hi: h_scr[lo:hi, cols]
        for k in range(1, levels + 1):
            lo = POOL_HALO - 8 * (levels - k)
            shift = 2 ** (k - 1)
            tot = src(lo, end) + src(lo - shift, end - shift)
            if k < levels:
                buf = a_scr if k % 2 else b_scr
                buf[lo:end, :] = tot
                src = lambda lo, hi, buf=buf: buf[lo:hi, :]
        cnt = jnp.minimum(t + 1, win).astype(F32)
        y = (tot / cnt - h).astype(BF16)
        y = jnp.dot(y, w_ref[gi], preferred_element_type=F32)
        y_ref[:, cols] = x[:, cols] + (y + b_ref[:, cols]) * s_ref[:, cols]
    h_scr[0:POOL_HALO, :] = h_scr[tm:tm + POOL_HALO, :]


def _pool_mix(x, g, w, b, scale, *, tm=512):
    n_groups = len(POOL_WINDOWS)
    return pl.pallas_call(
        functools.partial(_pool_kernel, tm=tm),
        grid=(SEQ // tm,),
        in_specs=[
            pl.BlockSpec((tm, D_MODEL), lambda i: (i, 0)),
            pl.BlockSpec((1, D_MODEL), lambda i: (0, 0)),
            pl.BlockSpec((n_groups, POOL_GROUP, POOL_GROUP), lambda i: (0, 0, 0)),
            pl.BlockSpec((1, D_MODEL), lambda i: (0, 0)),
            pl.BlockSpec((1, D_MODEL), lambda i: (0, 0)),
        ],
        out_specs=pl.BlockSpec((tm, D_MODEL), lambda i: (i, 0)),
        out_shape=jax.ShapeDtypeStruct((SEQ, D_MODEL), F32),
        scratch_shapes=[pltpu.VMEM((POOL_HALO + tm, D_MODEL), F32),
                        pltpu.VMEM((POOL_HALO + tm, POOL_GROUP), F32),
                        pltpu.VMEM((POOL_HALO + tm, POOL_GROUP), F32)],
        compiler_params=_params(("arbitrary",)),
        name="pool_mix",
    )(x, g, w, b, scale)


def _ffn_kernel(*refs, convert_next):
    if convert_next:
        (x_ref, g_ref, wg_ref, wu_ref, wd_ref, ngu_ref, ndn_ref,
         y_ref, ngu_out, ndn_out, h_scr) = refs
        ngu_out[...] = ngu_ref[...].astype(BF16)
        ndn_out[...] = ndn_ref[...].astype(BF16)
    else:
        x_ref, g_ref, wg_ref, wu_ref, wd_ref, y_ref, h_scr = refs
    j = pl.program_id(1)

    @pl.when(j == 0)
    def _():
        x = x_ref[...]
        h_scr[...] = _rmsnorm(x, g_ref[...]).astype(BF16)
        y_ref[...] = x

    h = h_scr[...]
    gate = jnp.dot(h, wg_ref[...], preferred_element_type=F32)
    up = jnp.dot(h, wu_ref[...], preferred_element_type=F32)
    a = (gate * jax.nn.sigmoid(gate) * up).astype(BF16)
    y_ref[...] += jnp.dot(a, wd_ref[...], preferred_element_type=F32)


def _ffn(x, g, w_gu, w_down, next_w=None, *, tm=1024, tf=512):
    nf = D_FF // tf
    steps = (SEQ // tm) * nf
    in_specs = [
        pl.BlockSpec((tm, D_MODEL), lambda i, j: (i, 0)),
        pl.BlockSpec((1, D_MODEL), lambda i, j: (0, 0)),
        pl.BlockSpec((D_MODEL, tf), lambda i, j: (0, j)),
        pl.BlockSpec((D_MODEL, tf), lambda i, j: (0, nf + j)),
        pl.BlockSpec((tf, D_MODEL), lambda i, j: (j, 0)),
    ]
    out_specs = [pl.BlockSpec((tm, D_MODEL), lambda i, j: (i, 0))]
    out_shape = [jax.ShapeDtypeStruct((SEQ, D_MODEL), F32)]
    args = [x, g, w_gu, w_gu, w_down]
    if next_w is not None:
        n_gu, n_down, layer = next_w
        gu_cols = 2 * D_FF // steps
        dn_rows = D_FF // steps
        assert gu_cols % LANES == 0 and dn_rows % 16 == 0
        in_specs += [
            pl.BlockSpec((None, D_MODEL, gu_cols), lambda i, j: (layer, 0, i * nf + j)),
            pl.BlockSpec((None, dn_rows, D_MODEL), lambda i, j: (layer, i * nf + j, 0)),
        ]
        out_specs += [
            pl.BlockSpec((D_MODEL, gu_cols), lambda i, j: (0, i * nf + j)),
            pl.BlockSpec((dn_rows, D_MODEL), lambda i, j: (i * nf + j, 0)),
        ]
        out_shape += [jax.ShapeDtypeStruct((D_MODEL, 2 * D_FF), BF16),
                      jax.ShapeDtypeStruct((D_FF, D_MODEL), BF16)]
        args += [n_gu, n_down]
    return pl.pallas_call(
        functools.partial(_ffn_kernel, convert_next=next_w is not None),
        grid=(SEQ // tm, nf),
        in_specs=in_specs,
        out_specs=out_specs,
        out_shape=out_shape,
        scratch_shapes=[pltpu.VMEM((tm, D_MODEL), BF16)],
        compiler_params=_params(("arbitrary", "arbitrary")),
        name="ffn",
    )(*args)


def kernel(x, mix_norm_g, ffn_norm_g, fox_w_in, fox_b_f, fox_q_norm_g, fox_k_norm_g,
           fox_w_out, pool_w, pool_b, pool_scale, ffn_w_gate_up, ffn_w_down):
    assert x.shape == (1, SEQ, D_MODEL)
    x = x.reshape(SEQ, D_MODEL)
    tri = jnp.tril(jnp.ones((LANES, LANES), BF16))
    q_scale = HEAD_DIM ** -0.5 * LOG2E
    fox_w_in_t = jnp.swapaxes(fox_w_in, 1, 2)
    w_gu = w_down = None
    for i in range(DEPTH):
        j = i // 2
        g_mix = mix_norm_g[i][None, :]
        if i % 2 == 0:
            wf_t = jnp.pad(fox_w_in_t[j, 3 * D_MODEL:, :], ((0, LANES - N_HEADS), (0, 0)))
            bf = jnp.pad(fox_b_f[j], (0, LANES - N_HEADS))[None, :]
            gains = jnp.concatenate([
                jnp.tile(fox_q_norm_g[j] * q_scale, N_HEADS),
                jnp.tile(fox_k_norm_g[j], N_HEADS),
                jnp.ones((D_MODEL,), F32)])[None, :]
            qkv, ct = _fox_proj(x, g_mix, fox_w_in_t, j, gains, wf_t, bf, tri)
            qk_bound = (1.02 * HEAD_DIM * q_scale * jnp.max(jnp.abs(fox_q_norm_g[j]))
                        * jnp.max(jnp.abs(fox_k_norm_g[j])))
            par = jnp.stack([-(SKIP_EXP2 + 2.0 * qk_bound), qk_bound]).astype(F32)
            cend = jnp.pad(ct[:, ATTN_BLOCK - 1::ATTN_BLOCK],
                           ((0, 0), (0, LANES - SEQ // ATTN_BLOCK)))
            if i == 0:
                o, w_gu, w_down = _fox_attn(qkv, ct[:, None, :], cend[:, None, :], par,
                                            (ffn_w_gate_up, ffn_w_down, 0), t=ATTN_BLOCK)
            else:
                o, = _fox_attn(qkv, ct[:, None, :], cend[:, None, :], par, t=ATTN_BLOCK)
            x = _fox_out(x, o, fox_w_out, j)
        else:
            x = _pool_mix(x, g_mix, pool_w[j].astype(BF16), pool_b[j][None, :],
                          pool_scale[j][None, :])
        if i + 1 < DEPTH:
            x, w_gu, w_down = _ffn(x, ffn_norm_g[i][None, :], w_gu, w_down,
                                   (ffn_w_gate_up, ffn_w_down, i + 1))
        else:
            x, = _ffn(x, ffn_norm_g[i][None, :], w_gu, w_down)
    return x.reshape(1, SEQ, D_MODEL)
```

```python
import functools
import math

import jax
import jax.numpy as jnp
from jax import lax
from jax.experimental import pallas as pl
from jax.experimental.pallas import tpu as pltpu

D_MODEL = 2048
SEQ = 8192
DEPTH = 4
HEAD_DIM = 128
N_HEADS = D_MODEL // HEAD_DIM
POOL_WINDOWS = (2, 4, 8, 16)
POOL_GROUP = D_MODEL // len(POOL_WINDOWS)
POOL_HALO = 32
D_FF = 5632
RMS_EPS = 1e-6
NEG_INF = -1e30
LOG2E = math.log2(math.e)

LANES = 128
VMEM_LIMIT = 62 * 1024 * 1024

F32 = jnp.float32
BF16 = jnp.bfloat16


def _params(semantics):
    return pltpu.CompilerParams(dimension_semantics=semantics,
                                vmem_limit_bytes=VMEM_LIMIT)


def _rmsnorm(x, g):
    return x * lax.rsqrt(jnp.mean(x * x, axis=-1, keepdims=True) + RMS_EPS) * g


def _split3_bf16(x):
    x1 = x.astype(BF16)
    r1 = x - x1.astype(F32)
    x2 = r1.astype(BF16)
    x3 = (r1 - x2.astype(F32)).astype(BF16)
    return x1, x2, x3


ROW_CHUNK = 256
_NT_DIMS = (((1,), (1,)), ((), ()))


def _fox_proj_kernel(x_ref, g_ref, w_ref, gain_ref, wf_ref, bf_ref, tri_ref,
                     qkv_ref, ct_ref, h_scr, carry_scr, *, tm, tn):
    i = pl.program_id(0)
    j = pl.program_id(1)

    @pl.when(j == 0)
    def _():
        hb = _rmsnorm(x_ref[...], g_ref[...]).astype(BF16)
        h_scr[...] = hb
        f = lax.dot_general(hb, wf_ref[...].astype(BF16), _NT_DIMS,
                            preferred_element_type=F32) + bf_ref[...]
        logf = jnp.minimum(f, 0.0) - jnp.log1p(jnp.exp(-jnp.abs(f)))

        @pl.when(i == 0)
        def _():
            carry_scr[...] = jnp.zeros_like(carry_scr)

        tri = tri_ref[...]
        carry = carry_scr[0:1, :]
        chunks = []
        for r in range(tm // LANES):
            x1, x2, x3 = _split3_bf16(logf[r * LANES:(r + 1) * LANES, :])
            cs = (jnp.dot(tri, x1, preferred_element_type=F32)
                  + jnp.dot(tri, x2, preferred_element_type=F32)
                  + jnp.dot(tri, x3, preferred_element_type=F32)) + carry
            carry = cs[LANES - 1:LANES, :]
            chunks.append(cs)
        carry_scr[0:1, :] = carry
        c = jnp.concatenate(chunks, axis=0) * LOG2E
        ct_ref[...] = c.T[:N_HEADS, :]

    w_t = w_ref[...].astype(BF16)
    is_qk = (j < 2 * D_MODEL // tn).astype(F32)
    for rc in range(tm // ROW_CHUNK):
        rows = slice(rc * ROW_CHUNK, (rc + 1) * ROW_CHUNK)
        r = lax.dot_general(h_scr[rows, :], w_t, _NT_DIMS,
                            preferred_element_type=F32)
        for hh in range(tn // HEAD_DIM):
            sl = slice(hh * HEAD_DIM, (hh + 1) * HEAD_DIM)
            rh = r[:, sl]
            inv = lax.rsqrt(jnp.mean(rh * rh, axis=-1, keepdims=True) + RMS_EPS)
            scale = is_qk * inv + (1.0 - is_qk)
            qkv_ref[rows, sl] = (rh * scale * gain_ref[:, sl]).astype(BF16)


def _fox_proj(x, g, w_in_t, layer, gains, wf_t, bf, tri, *, tm=1024, tn=1024):
    n = 3 * D_MODEL
    return pl.pallas_call(
        functools.partial(_fox_proj_kernel, tm=tm, tn=tn),
        grid=(SEQ // tm, n // tn),
        in_specs=[
            pl.BlockSpec((tm, D_MODEL), lambda i, j: (i, 0)),
            pl.BlockSpec((1, D_MODEL), lambda i, j: (0, 0)),
            pl.BlockSpec((None, tn, D_MODEL), lambda i, j: (layer, j, 0)),
            pl.BlockSpec((1, tn), lambda i, j: (0, j)),
            pl.BlockSpec((LANES, D_MODEL), lambda i, j: (0, 0)),
            pl.BlockSpec((1, LANES), lambda i, j: (0, 0)),
            pl.BlockSpec((LANES, LANES), lambda i, j: (0, 0)),
        ],
        out_specs=[
            pl.BlockSpec((tm, tn), lambda i, j: (i, j)),
            pl.BlockSpec((N_HEADS, tm), lambda i, j: (0, i)),
        ],
        out_shape=[
            jax.ShapeDtypeStruct((SEQ, n), BF16),
            jax.ShapeDtypeStruct((N_HEADS, SEQ), F32),
        ],
        scratch_shapes=[
            pltpu.VMEM((tm, D_MODEL), BF16),
            pltpu.VMEM((8, LANES), F32),
        ],
        compiler_params=_params(("arbitrary", "arbitrary")),
        name="fox_proj",
    )(x, g, w_in_t, gains, wf_t, bf, tri)


SKIP_EXP2 = 160.0
ATTN_BLOCK = 512
MAX_FIXED_SHIFT_BOUND = 48.0


def _fox_attn_kernel(*refs, t, convert):
    if convert:
        (par_ref, q_ref, k_ref, v_ref, c_ref, cend_ref, wa_ref, wb_ref,
         o_ref, wa_out, wb_out, m_scr, l_scr, acc_scr) = refs
        wa_out[...] = wa_ref[...].astype(BF16)
        wb_out[...] = wb_ref[...].astype(BF16)
    else:
        (par_ref, q_ref, k_ref, v_ref, c_ref, cend_ref,
         o_ref, m_scr, l_scr, acc_scr) = refs
    blk = lax.broadcasted_iota(jnp.int32, (1, LANES), 1)
    cend = cend_ref[0]
    thr = par_ref[0]
    qk_bound = par_ref[1]

    def causal(x):
        row = lax.broadcasted_iota(jnp.int32, (t, t), 0)
        col = lax.broadcasted_iota(jnp.int32, (t, t), 1)
        return jnp.where(row >= col, x, NEG_INF)

    def scores(q, c_q0, j):
        k0 = pl.multiple_of(j * t, t)
        s = lax.dot_general(q, k_ref[pl.ds(k0, t), :], _NT_DIMS,
                            preferred_element_type=F32)
        return s + (c_q0 - c_ref[0, :, pl.ds(k0, t)])

    def v_block(j):
        return v_ref[pl.ds(pl.multiple_of(j * t, t), t), :]

    def q_block(qi, fixed_shift):
        q0 = pl.multiple_of(qi * t, t)
        q = q_ref[pl.ds(q0, t), :]
        c_q = c_ref[0, :, pl.ds(q0, t)]
        c_q0 = c_q[:, 0:1]
        skippable = ((c_q0 - cend) < thr) & (blk < qi)
        first = jnp.sum(skippable.astype(jnp.int32))
        l_scr[...] = jnp.zeros_like(l_scr)
        acc_scr[...] = jnp.zeros_like(acc_scr)

        if fixed_shift:
            shift = qk_bound + (c_q0 - c_q)
            shift = jnp.broadcast_to(shift, (LANES, t)).T
            shift = jnp.concatenate([shift] * (t // LANES), axis=1)

            def step(j, masked):
                x = scores(q, c_q0, j) - shift
                p = jnp.exp2(causal(x) if masked else x)
                part = p[:, 0:LANES]
                for n in range(1, t // LANES):
                    part = part + p[:, n * LANES:(n + 1) * LANES]
                l_scr[...] += part
                acc_scr[...] += jnp.dot(p.astype(BF16), v_block(j),
                                        preferred_element_type=F32)
        else:
            m_scr[...] = jnp.full_like(m_scr, NEG_INF)

            def step(j, masked):
                s = scores(q, c_q0, j)
                if masked:
                    s = causal(s)
                m_prev = m_scr[...]
                m_next = jnp.maximum(m_prev, jnp.max(s, axis=1, keepdims=True))
                p = jnp.exp2(s - jnp.concatenate([m_next] * (t // LANES), axis=1))
                alpha = jnp.exp2(m_prev - m_next)
                l_scr[...] = alpha * l_scr[...] + jnp.sum(p, axis=1, keepdims=True)
                m_scr[...] = m_next
                acc_scr[...] = alpha * acc_scr[...] + jnp.dot(
                    p.astype(BF16), v_block(j), preferred_element_type=F32)

        n_pairs = (qi - first) // 2

        def kv_pair(n, c):
            j = first + 2 * n
            step(j, False)
            step(j + 1, False)
            return c

        lax.fori_loop(0, n_pairs, kv_pair, 0)

        @pl.when(first + 2 * n_pairs < qi)
        def _():
            step(qi - 1, False)

        step(qi, True)
        l = l_scr[...]
        if fixed_shift:
            l = jnp.sum(l, axis=1, keepdims=True)
        o_ref[pl.ds(q0, t), :] = (acc_scr[...] / l).astype(BF16)

    def all_q_blocks(fixed_shift):
        def body(qi, carry):
            q_block(qi, fixed_shift)
            return carry
        lax.fori_loop(0, SEQ // t, body, 0)

    @pl.when(qk_bound < MAX_FIXED_SHIFT_BOUND)
    def _():
        all_q_blocks(True)

    @pl.when(qk_bound >= MAX_FIXED_SHIFT_BOUND)
    def _():
        all_q_blocks(False)


def _fox_attn(qkv, c3, cend3, par, cast_w=None, *, t=512):
    in_specs = [
        pl.BlockSpec(memory_space=pltpu.SMEM),
        pl.BlockSpec((SEQ, HEAD_DIM), lambda h: (0, h)),
        pl.BlockSpec((SEQ, HEAD_DIM), lambda h: (0, N_HEADS + h)),
        pl.BlockSpec((SEQ, HEAD_DIM), lambda h: (0, 2 * N_HEADS + h)),
        pl.BlockSpec((1, 1, SEQ), lambda h: (h, 0, 0)),
        pl.BlockSpec((1, 1, LANES), lambda h: (h, 0, 0)),
    ]
    out_specs = [pl.BlockSpec((SEQ, HEAD_DIM), lambda h: (0, h))]
    out_shape = [jax.ShapeDtypeStruct((SEQ, D_MODEL), BF16)]
    args = [par, qkv, qkv, qkv, c3, cend3]
    if cast_w is not None:
        wa, wb, layer = cast_w
        for w in (wa, wb):
            rows, cols = w.shape[1] // N_HEADS, w.shape[2]
            assert rows % 16 == 0
            in_specs.append(pl.BlockSpec((None, rows, cols), lambda h: (layer, h, 0)))
            out_specs.append(pl.BlockSpec((rows, cols), lambda h: (h, 0)))
            out_shape.append(jax.ShapeDtypeStruct(w.shape[1:], BF16))
        args += [wa, wb]
    return pl.pallas_call(
        functools.partial(_fox_attn_kernel, t=t, convert=cast_w is not None),
        grid=(N_HEADS,),
        in_specs=in_specs,
        out_specs=out_specs,
        out_shape=out_shape,
        scratch_shapes=[
            pltpu.VMEM((t, LANES), F32),
            pltpu.VMEM((t, LANES), F32),
            pltpu.VMEM((t, HEAD_DIM), F32),
        ],
        compiler_params=_params(("arbitrary",)),
        name="fox_attn",
    )(*args)


def _fox_out_kernel(x_ref, o_ref, w_ref, y_ref, wb_scr):
    @pl.when(pl.program_id(0) == 0)
    def _():
        wb_scr[...] = w_ref[...].astype(BF16)

    y_ref[...] = x_ref[...] + jnp.dot(o_ref[...], wb_scr[...],
                                      preferred_element_type=F32)


def _fox_out(x, o, w_out, layer, *, tm=512):
    return pl.pallas_call(
        _fox_out_kernel,
        grid=(SEQ // tm,),
        in_specs=[
            pl.BlockSpec((tm, D_MODEL), lambda i: (i, 0)),
            pl.BlockSpec((tm, D_MODEL), lambda i: (i, 0)),
            pl.BlockSpec((None, D_MODEL, D_MODEL), lambda i: (layer, 0, 0),
                         pipeline_mode=pl.Buffered(1)),
        ],
        out_specs=pl.BlockSpec((tm, D_MODEL), lambda i: (i, 0)),
        out_shape=jax.ShapeDtypeStruct((SEQ, D_MODEL), F32),
        scratch_shapes=[pltpu.VMEM((D_MODEL, D_MODEL), BF16)],
        compiler_params=_params(("arbitrary",)),
        name="fox_out",
    )(x, o, w_out)


def _pool_kernel(x_ref, g_ref, w_ref, b_ref, s_ref, y_ref, h_scr, a_scr, b_scr, *, tm):
    i = pl.program_id(0)
    end = POOL_HALO + tm

    @pl.when(i == 0)
    def _():
        h_scr[0:POOL_HALO, :] = jnp.zeros((POOL_HALO, D_MODEL), F32)

    x = x_ref[...]
    h_scr[POOL_HALO:end, :] = _rmsnorm(x, g_ref[...])
    t = i * tm + lax.broadcasted_iota(jnp.int32, (tm, 1), 0)
    for gi, win in enumerate(POOL_WINDOWS):
        cols = slice(gi * POOL_GROUP, (gi + 1) * POOL_GROUP)
        h = h_scr[POOL_HALO:end, cols]
        levels = win.bit_length() - 1
        src = lambda lo, hi: h_scr[lo:hi, cols]
        for k in range(1, levels + 1):
            lo = POOL_HALO - 8 * (levels - k)
            shift = 2 ** (k - 1)
            tot = src(lo, end) + src(lo - shift, end - shift)
            if k < levels:
                buf = a_scr if k % 2 else b_scr
                buf[lo:end, :] = tot
                src = lambda lo, hi, buf=buf: buf[lo:hi, :]
        cnt = jnp.minimum(t + 1, win).astype(F32)
        y = (tot / cnt - h).astype(BF16)
        y = jnp.dot(y, w_ref[gi], preferred_element_type=F32)
        y_ref[:, cols] = x[:, cols] + (y + b_ref[:, cols]) * s_ref[:, cols]
    h_scr[0:POOL_HALO, :] = h_scr[tm:tm + POOL_HALO, :]


def _pool_mix(x, g, w, b, scale, *, tm=512):
    n_groups = len(POOL_WINDOWS)
    return pl.pallas_call(
        functools.partial(_pool_kernel, tm=tm),
        grid=(SEQ // tm,),
        in_specs=[
            pl.BlockSpec((tm, D_MODEL), lambda i: (i, 0)),
            pl.BlockSpec((1, D_MODEL), lambda i: (0, 0)),
            pl.BlockSpec((n_groups, POOL_GROUP, POOL_GROUP), lambda i: (0, 0, 0)),
            pl.BlockSpec((1, D_MODEL), lambda i: (0, 0)),
            pl.BlockSpec((1, D_MODEL), lambda i: (0, 0)),
        ],
        out_specs=pl.BlockSpec((tm, D_MODEL), lambda i: (i, 0)),
        out_shape=jax.ShapeDtypeStruct((SEQ, D_MODEL), F32),
        scratch_shapes=[pltpu.VMEM((POOL_HALO + tm, D_MODEL), F32),
                        pltpu.VMEM((POOL_HALO + tm, POOL_GROUP), F32),
                        pltpu.VMEM((POOL_HALO + tm, POOL_GROUP), F32)],
        compiler_params=_params(("arbitrary",)),
        name="pool_mix",
    )(x, g, w, b, scale)


FFN_ROW_CHUNK = 256


def _ffn_kernel(*refs, convert_next):
    if convert_next:
        (x_ref, g_ref, wg_ref, wu_ref, wd_ref, ngu_ref, ndn_ref,
         y_ref, ngu_out, ndn_out, h_scr) = refs
        ngu_out[...] = ngu_ref[...].astype(BF16)
        ndn_out[...] = ndn_ref[...].astype(BF16)
    else:
        x_ref, g_ref, wg_ref, wu_ref, wd_ref, y_ref, h_scr = refs
    j = pl.program_id(1)

    @pl.when(j == 0)
    def _():
        x = x_ref[...]
        h_scr[...] = _rmsnorm(x, g_ref[...]).astype(BF16)
        y_ref[...] = x

    chunks = [slice(r, r + FFN_ROW_CHUNK) for r in range(0, x_ref.shape[0], FFN_ROW_CHUNK)]
    acts = []
    for rows in chunks:
        h = h_scr[rows, :]
        gate = jnp.dot(h, wg_ref[...], preferred_element_type=F32)
        up = jnp.dot(h, wu_ref[...], preferred_element_type=F32)
        acts.append((gate * jax.nn.sigmoid(gate) * up).astype(BF16))
    for rows, a in zip(chunks, acts):
        y_ref[rows, :] += jnp.dot(a, wd_ref[...], preferred_element_type=F32)


def _ffn(x, g, w_gu, w_down, next_w=None, *, tm=1024, tf=512):
    nf = D_FF // tf
    steps = (SEQ // tm) * nf
    in_specs = [
        pl.BlockSpec((tm, D_MODEL), lambda i, j: (i, 0)),
        pl.BlockSpec((1, D_MODEL), lambda i, j: (0, 0)),
        pl.BlockSpec((D_MODEL, tf), lambda i, j: (0, j)),
        pl.BlockSpec((D_MODEL, tf), lambda i, j: (0, nf + j)),
        pl.BlockSpec((tf, D_MODEL), lambda i, j: (j, 0)),
    ]
    out_specs = [pl.BlockSpec((tm, D_MODEL), lambda i, j: (i, 0))]
    out_shape = [jax.ShapeDtypeStruct((SEQ, D_MODEL), F32)]
    args = [x, g, w_gu, w_gu, w_down]
    if next_w is not None:
        n_gu, n_down, layer = next_w
        gu_cols = 2 * D_FF // steps
        dn_rows = D_FF // steps
        assert gu_cols % LANES == 0 and dn_rows % 16 == 0
        in_specs += [
            pl.BlockSpec((None, D_MODEL, gu_cols), lambda i, j: (layer, 0, i * nf + j)),
            pl.BlockSpec((None, dn_rows, D_MODEL), lambda i, j: (layer, i * nf + j, 0)),
        ]
        out_specs += [
            pl.BlockSpec((D_MODEL, gu_cols), lambda i, j: (0, i * nf + j)),
            pl.BlockSpec((dn_rows, D_MODEL), lambda i, j: (i * nf + j, 0)),
        ]
        out_shape += [jax.ShapeDtypeStruct((D_MODEL, 2 * D_FF), BF16),
                      jax.ShapeDtypeStruct((D_FF, D_MODEL), BF16)]
        args += [n_gu, n_down]
    return pl.pallas_call(
        functools.partial(_ffn_kernel, convert_next=next_w is not None),
        grid=(SEQ // tm, nf),
        in_specs=in_specs,
        out_specs=out_specs,
        out_shape=out_shape,
        scratch_shapes=[pltpu.VMEM((tm, D_MODEL), BF16)],
        compiler_params=_params(("arbitrary", "arbitrary")),
        name="ffn",
    )(*args)


def kernel(x, mix_norm_g, ffn_norm_g, fox_w_in, fox_b_f, fox_q_norm_g, fox_k_norm_g,
           fox_w_out, pool_w, pool_b, pool_scale, ffn_w_gate_up, ffn_w_down):
    assert x.shape == (1, SEQ, D_MODEL)
    x = x.reshape(SEQ, D_MODEL)
    tri = jnp.tril(jnp.ones((LANES, LANES), BF16))
    q_scale = HEAD_DIM ** -0.5 * LOG2E
    fox_w_in_t = jnp.swapaxes(fox_w_in, 1, 2)
    w_gu = w_down = None
    for i in range(DEPTH):
        j = i // 2
        g_mix = mix_norm_g[i][None, :]
        if i % 2 == 0:
            wf_t = jnp.pad(fox_w_in_t[j, 3 * D_MODEL:, :], ((0, LANES - N_HEADS), (0, 0)))
            bf = jnp.pad(fox_b_f[j], (0, LANES - N_HEADS))[None, :]
            gains = jnp.concatenate([
                jnp.tile(fox_q_norm_g[j] * q_scale, N_HEADS),
                jnp.tile(fox_k_norm_g[j], N_HEADS),
                jnp.ones((D_MODEL,), F32)])[None, :]
            qkv, ct = _fox_proj(x, g_mix, fox_w_in_t, j, gains, wf_t, bf, tri)
            qk_bound = (1.02 * HEAD_DIM * q_scale * jnp.max(jnp.abs(fox_q_norm_g[j]))
                        * jnp.max(jnp.abs(fox_k_norm_g[j])))
            par = jnp.stack([-(SKIP_EXP2 + 2.0 * qk_bound), qk_bound]).astype(F32)
            cend = jnp.pad(ct[:, ATTN_BLOCK - 1::ATTN_BLOCK],
                           ((0, 0), (0, LANES - SEQ // ATTN_BLOCK)))
            if i == 0:
                o, w_gu, w_down = _fox_attn(qkv, ct[:, None, :], cend[:, None, :], par,
                                            (ffn_w_gate_up, ffn_w_down, 0), t=ATTN_BLOCK)
            else:
                o, = _fox_attn(qkv, ct[:, None, :], cend[:, None, :], par, t=ATTN_BLOCK)
            x = _fox_out(x, o, fox_w_out, j)
        else:
            x = _pool_mix(x, g_mix, pool_w[j].astype(BF16), pool_b[j][None, :],
                          pool_scale[j][None, :])
        if i + 1 < DEPTH:
            x, w_gu, w_down = _ffn(x, ffn_norm_g[i][None, :], w_gu, w_down,
                                   (ffn_w_gate_up, ffn_w_down, i + 1))
        else:
            x, = _ffn(x, ffn_norm_g[i][None, :], w_gu, w_down)
    return x.reshape(1, SEQ, D_MODEL)
```

```python
import functools
import math

import jax
import jax.numpy as jnp
from jax import lax
from jax.experimental import pallas as pl
from jax.experimental.pallas import tpu as pltpu

D_MODEL = 2048
SEQ = 8192
DEPTH = 4
HEAD_DIM = 128
N_HEADS = D_MODEL // HEAD_DIM
POOL_WINDOWS = (2, 4, 8, 16)
POOL_GROUP = D_MODEL // len(POOL_WINDOWS)
POOL_HALO = 32
D_FF = 5632
RMS_EPS = 1e-6
NEG_INF = -1e30
LOG2E = math.log2(math.e)

LANES = 128
VMEM_LIMIT = 62 * 1024 * 1024

F32 = jnp.float32
BF16 = jnp.bfloat16


def _params(semantics):
    return pltpu.CompilerParams(dimension_semantics=semantics,
                                vmem_limit_bytes=VMEM_LIMIT)


def _pipelined_call(body, *, grid, in_specs, out_specs, out_shape, scratch_shapes, name):
    n_in, n_out = len(in_specs), len(out_specs)

    def outer(*refs):
        pltpu.emit_pipeline(body, grid=grid, in_specs=in_specs, out_specs=out_specs)(
            *refs[:n_in + n_out], scratches=refs[n_in + n_out:])

    any_spec = pl.BlockSpec(memory_space=pl.ANY)
    return pl.pallas_call(
        outer,
        in_specs=[any_spec] * n_in,
        out_specs=[any_spec] * n_out,
        out_shape=out_shape,
        scratch_shapes=scratch_shapes,
        compiler_params=pltpu.CompilerParams(vmem_limit_bytes=VMEM_LIMIT),
        name=name,
    )


def _rmsnorm(x, g):
    return x * lax.rsqrt(jnp.mean(x * x, axis=-1, keepdims=True) + RMS_EPS) * g


def _split3_bf16(x):
    x1 = x.astype(BF16)
    r1 = x - x1.astype(F32)
    x2 = r1.astype(BF16)
    x3 = (r1 - x2.astype(F32)).astype(BF16)
    return x1, x2, x3


ROW_CHUNK = 256
_NT_DIMS = (((1,), (1,)), ((), ()))


def _fox_proj_kernel(x_ref, g_ref, w_ref, gain_ref, wf_ref, bf_ref, tri_ref,
                     qkv_ref, ct_ref, h_scr, carry_scr, *, tm, tn):
    i = pl.program_id(0)
    j = pl.program_id(1)

    @pl.when(j == 0)
    def _():
        hb = _rmsnorm(x_ref[...], g_ref[...]).astype(BF16)
        h_scr[...] = hb
        f = lax.dot_general(hb, wf_ref[...].astype(BF16), _NT_DIMS,
                            preferred_element_type=F32) + bf_ref[...]
        logf = jnp.minimum(f, 0.0) - jnp.log1p(jnp.exp(-jnp.abs(f)))

        @pl.when(i == 0)
        def _():
            carry_scr[...] = jnp.zeros_like(carry_scr)

        tri = tri_ref[...]
        carry = carry_scr[0:1, :]
        chunks = []
        for r in range(tm // LANES):
            x1, x2, x3 = _split3_bf16(logf[r * LANES:(r + 1) * LANES, :])
            cs = (jnp.dot(tri, x1, preferred_element_type=F32)
                  + jnp.dot(tri, x2, preferred_element_type=F32)
                  + jnp.dot(tri, x3, preferred_element_type=F32)) + carry
            carry = cs[LANES - 1:LANES, :]
            chunks.append(cs)
        carry_scr[0:1, :] = carry
        c = jnp.concatenate(chunks, axis=0) * LOG2E
        ct_ref[...] = c.T[:N_HEADS, :]

    w_t = w_ref[...].astype(BF16)
    is_qk = (j < 2 * D_MODEL // tn).astype(F32)
    for rc in range(tm // ROW_CHUNK):
        rows = slice(rc * ROW_CHUNK, (rc + 1) * ROW_CHUNK)
        r = lax.dot_general(h_scr[rows, :], w_t, _NT_DIMS,
                            preferred_element_type=F32)
        for hh in range(tn // HEAD_DIM):
            sl = slice(hh * HEAD_DIM, (hh + 1) * HEAD_DIM)
            rh = r[:, sl]
            inv = lax.rsqrt(jnp.mean(rh * rh, axis=-1, keepdims=True) + RMS_EPS)
            scale = is_qk * inv + (1.0 - is_qk)
            qkv_ref[rows, sl] = (rh * scale * gain_ref[:, sl]).astype(BF16)


def _fox_proj(x, g, w_in_t, layer, gains, wf_t, bf, tri, *, tm=1024, tn=1024):
    n = 3 * D_MODEL
    return pl.pallas_call(
        functools.partial(_fox_proj_kernel, tm=tm, tn=tn),
        grid=(SEQ // tm, n // tn),
        in_specs=[
            pl.BlockSpec((tm, D_MODEL), lambda i, j: (i, 0)),
            pl.BlockSpec((1, D_MODEL), lambda i, j: (0, 0)),
            pl.BlockSpec((None, tn, D_MODEL), lambda i, j: (layer, j, 0)),
            pl.BlockSpec((1, tn), lambda i, j: (0, j)),
            pl.BlockSpec((LANES, D_MODEL), lambda i, j: (0, 0)),
            pl.BlockSpec((1, LANES), lambda i, j: (0, 0)),
            pl.BlockSpec((LANES, LANES), lambda i, j: (0, 0)),
        ],
        out_specs=[
            pl.BlockSpec((tm, tn), lambda i, j: (i, j)),
            pl.BlockSpec((N_HEADS, tm), lambda i, j: (0, i)),
        ],
        out_shape=[
            jax.ShapeDtypeStruct((SEQ, n), BF16),
            jax.ShapeDtypeStruct((N_HEADS, SEQ), F32),
        ],
        scratch_shapes=[
            pltpu.VMEM((tm, D_MODEL), BF16),
            pltpu.VMEM((8, LANES), F32),
        ],
        compiler_params=_params(("arbitrary", "arbitrary")),
        name="fox_proj",
    )(x, g, w_in_t, gains, wf_t, bf, tri)


SKIP_EXP2 = 160.0
ATTN_BLOCK = 512
MAX_FIXED_SHIFT_BOUND = 48.0


def _fox_attn_kernel(*refs, t, convert):
    if convert:
        (par_ref, q_ref, k_ref, v_ref, c_ref, cend_ref, wa_ref, wb_ref,
         o_ref, wa_out, wb_out, m_scr, l_scr, acc_scr) = refs
        wa_out[...] = wa_ref[...].astype(BF16)
        wb_out[...] = wb_ref[...].astype(BF16)
    else:
        (par_ref, q_ref, k_ref, v_ref, c_ref, cend_ref,
         o_ref, m_scr, l_scr, acc_scr) = refs
    blk = lax.broadcasted_iota(jnp.int32, (1, LANES), 1)
    cend = cend_ref[0]
    thr = par_ref[0]
    qk_bound = par_ref[1]

    def causal(x):
        row = lax.broadcasted_iota(jnp.int32, (t, t), 0)
        col = lax.broadcasted_iota(jnp.int32, (t, t), 1)
        return jnp.where(row >= col, x, NEG_INF)

    def scores(q, c_q0, j):
        k0 = pl.multiple_of(j * t, t)
        s = lax.dot_general(q, k_ref[pl.ds(k0, t), :], _NT_DIMS,
                            preferred_element_type=F32)
        return s + (c_q0 - c_ref[0, :, pl.ds(k0, t)])

    def v_block(j):
        return v_ref[pl.ds(pl.multiple_of(j * t, t), t), :]

    def q_block(qi, fixed_shift):
        q0 = pl.multiple_of(qi * t, t)
        q = q_ref[pl.ds(q0, t), :]
        c_q = c_ref[0, :, pl.ds(q0, t)]
        c_q0 = c_q[:, 0:1]
        skippable = ((c_q0 - cend) < thr) & (blk < qi)
        first = jnp.sum(skippable.astype(jnp.int32))
        l_scr[...] = jnp.zeros_like(l_scr)
        acc_scr[...] = jnp.zeros_like(acc_scr)

        if fixed_shift:
            shift = qk_bound + (c_q0 - c_q)
            shift = jnp.broadcast_to(shift, (LANES, t)).T
            shift = jnp.concatenate([shift] * (t // LANES), axis=1)

            def step(j, masked):
                x = scores(q, c_q0, j) - shift
                p = jnp.exp2(causal(x) if masked else x)
                part = p[:, 0:LANES]
                for n in range(1, t // LANES):
                    part = part + p[:, n * LANES:(n + 1) * LANES]
                l_scr[...] += part
                acc_scr[...] += jnp.dot(p.astype(BF16), v_block(j),
                                        preferred_element_type=F32)
        else:
            m_scr[...] = jnp.full_like(m_scr, NEG_INF)

            def step(j, masked):
                s = scores(q, c_q0, j)
                if masked:
                    s = causal(s)
                m_prev = m_scr[...]
                m_next = jnp.maximum(m_prev, jnp.max(s, axis=1, keepdims=True))
                p = jnp.exp2(s - jnp.concatenate([m_next] * (t // LANES), axis=1))
                alpha = jnp.exp2(m_prev - m_next)
                l_scr[...] = alpha * l_scr[...] + jnp.sum(p, axis=1, keepdims=True)
                m_scr[...] = m_next
                acc_scr[...] = alpha * acc_scr[...] + jnp.dot(
                    p.astype(BF16), v_block(j), preferred_element_type=F32)

        n_pairs = (qi - first) // 2

        def kv_pair(n, c):
            j = first + 2 * n
            step(j, False)
            step(j + 1, False)
            return c

        lax.fori_loop(0, n_pairs, kv_pair, 0)

        @pl.when(first + 2 * n_pairs < qi)
        def _():
            step(qi - 1, False)

        step(qi, True)
        l = l_scr[...]
        if fixed_shift:
            l = jnp.sum(l, axis=1, keepdims=True)
        o_ref[pl.ds(q0, t), :] = (acc_scr[...] / l).astype(BF16)

    def all_q_blocks(fixed_shift):
        def body(qi, carry):
            q_block(qi, fixed_shift)
            return carry
        lax.fori_loop(0, SEQ // t, body, 0)

    @pl.when(qk_bound < MAX_FIXED_SHIFT_BOUND)
    def _():
        all_q_blocks(True)

    @pl.when(qk_bound >= MAX_FIXED_SHIFT_BOUND)
    def _():
        all_q_blocks(False)


def _fox_attn(qkv, c3, cend3, par, cast_w=None, *, t=512):
    in_specs = [
        pl.BlockSpec(memory_space=pltpu.SMEM),
        pl.BlockSpec((SEQ, HEAD_DIM), lambda h: (0, h)),
        pl.BlockSpec((SEQ, HEAD_DIM), lambda h: (0, N_HEADS + h)),
        pl.BlockSpec((SEQ, HEAD_DIM), lambda h: (0, 2 * N_HEADS + h)),
        pl.BlockSpec((1, 1, SEQ), lambda h: (h, 0, 0)),
        pl.BlockSpec((1, 1, LANES), lambda h: (h, 0, 0)),
    ]
    out_specs = [pl.BlockSpec((SEQ, HEAD_DIM), lambda h: (0, h))]
    out_shape = [jax.ShapeDtypeStruct((SEQ, D_MODEL), BF16)]
    args = [par, qkv, qkv, qkv, c3, cend3]
    if cast_w is not None:
        wa, wb, layer = cast_w
        for w in (wa, wb):
            rows, cols = w.shape[1] // N_HEADS, w.shape[2]
            assert rows % 16 == 0
            in_specs.append(pl.BlockSpec((None, rows, cols), lambda h: (layer, h, 0)))
            out_specs.append(pl.BlockSpec((rows, cols), lambda h: (h, 0)))
            out_shape.append(jax.ShapeDtypeStruct(w.shape[1:], BF16))
        args += [wa, wb]
    return pl.pallas_call(
        functools.partial(_fox_attn_kernel, t=t, convert=cast_w is not None),
        grid=(N_HEADS,),
        in_specs=in_specs,
        out_specs=out_specs,
        out_shape=out_shape,
        scratch_shapes=[
            pltpu.VMEM((t, LANES), F32),
            pltpu.VMEM((t, LANES), F32),
            pltpu.VMEM((t, HEAD_DIM), F32),
        ],
        compiler_params=_params(("arbitrary",)),
        name="fox_attn",
    )(*args)


def _fox_out_kernel(x_ref, o_ref, w_ref, y_ref, wb_scr):
    @pl.when(pl.program_id(0) == 0)
    def _():
        wb_scr[...] = w_ref[...].astype(BF16)

    y_ref[...] = x_ref[...] + jnp.dot(o_ref[...], wb_scr[...],
                                      preferred_element_type=F32)


def _fox_out(x, o, w_out, layer, *, tm=512):
    return pl.pallas_call(
        _fox_out_kernel,
        grid=(SEQ // tm,),
        in_specs=[
            pl.BlockSpec((tm, D_MODEL), lambda i: (i, 0)),
            pl.BlockSpec((tm, D_MODEL), lambda i: (i, 0)),
            pl.BlockSpec((None, D_MODEL, D_MODEL), lambda i: (layer, 0, 0),
                         pipeline_mode=pl.Buffered(1)),
        ],
        out_specs=pl.BlockSpec((tm, D_MODEL), lambda i: (i, 0)),
        out_shape=jax.ShapeDtypeStruct((SEQ, D_MODEL), F32),
        scratch_shapes=[pltpu.VMEM((D_MODEL, D_MODEL), BF16)],
        compiler_params=_params(("arbitrary",)),
        name="fox_out",
    )(x, o, w_out)


def _pool_kernel(x_ref, g_ref, w_ref, b_ref, s_ref, y_ref, h_scr, a_scr, b_scr, *, tm):
    i = pl.program_id(0)
    end = POOL_HALO + tm

    @pl.when(i == 0)
    def _():
        h_scr[0:POOL_HALO, :] = jnp.zeros((POOL_HALO, D_MODEL), F32)

    x = x_ref[...]
    h_scr[POOL_HALO:end, :] = _rmsnorm(x, g_ref[...])
    t = i * tm + lax.broadcasted_iota(jnp.int32, (tm, 1), 0)
    for gi, win in enumerate(POOL_WINDOWS):
        cols = slice(gi * POOL_GROUP, (gi + 1) * POOL_GROUP)
        h = h_scr[POOL_HALO:end, cols]
        levels = win.bit_length() - 1
        src = lambda lo, hi: h_scr[lo:hi, cols]
        for k in range(1, levels + 1):
            lo = POOL_HALO - 8 * (levels - k)
            shift = 2 ** (k - 1)
            tot = src(lo, end) + src(lo - shift, end - shift)
            if k < levels:
                buf = a_scr if k % 2 else b_scr
                buf[lo:end, :] = tot
                src = lambda lo, hi, buf=buf: buf[lo:hi, :]
        cnt = jnp.minimum(t + 1, win).astype(F32)
        y = (tot / cnt - h).astype(BF16)
        y = jnp.dot(y, w_ref[gi], preferred_element_type=F32)
        y_ref[:, cols] = x[:, cols] + (y + b_ref[:, cols]) * s_ref[:, cols]
    h_scr[0:POOL_HALO, :] = h_scr[tm:tm + POOL_HALO, :]


def _pool_mix(x, g, w, b, scale, *, tm=512):
    n_groups = len(POOL_WINDOWS)
    return pl.pallas_call(
        functools.partial(_pool_kernel, tm=tm),
        grid=(SEQ // tm,),
        in_specs=[
            pl.BlockSpec((tm, D_MODEL), lambda i: (i, 0)),
            pl.BlockSpec((1, D_MODEL), lambda i: (0, 0)),
            pl.BlockSpec((n_groups, POOL_GROUP, POOL_GROUP), lambda i: (0, 0, 0)),
            pl.BlockSpec((1, D_MODEL), lambda i: (0, 0)),
            pl.BlockSpec((1, D_MODEL), lambda i: (0, 0)),
        ],
        out_specs=pl.BlockSpec((tm, D_MODEL), lambda i: (i, 0)),
        out_shape=jax.ShapeDtypeStruct((SEQ, D_MODEL), F32),
        scratch_shapes=[pltpu.VMEM((POOL_HALO + tm, D_MODEL), F32),
                        pltpu.VMEM((POOL_HALO + tm, POOL_GROUP), F32),
                        pltpu.VMEM((POOL_HALO + tm, POOL_GROUP), F32)],
        compiler_params=_params(("arbitrary",)),
        name="pool_mix",
    )(x, g, w, b, scale)


FFN_ROW_CHUNK = 256


def _ffn_kernel(*refs, convert_next):
    if convert_next:
        (x_ref, g_ref, wg_ref, wu_ref, wd_ref, ngu_ref, ndn_ref,
         y_ref, ngu_out, ndn_out, h_scr) = refs
        ngu_out[...] = ngu_ref[...].astype(BF16)
        ndn_out[...] = ndn_ref[...].astype(BF16)
    else:
        x_ref, g_ref, wg_ref, wu_ref, wd_ref, y_ref, h_scr = refs
    j = pl.program_id(1)

    @pl.when(j == 0)
    def _():
        x = x_ref[...]
        h_scr[...] = _rmsnorm(x, g_ref[...]).astype(BF16)
        y_ref[...] = x

    chunks = [slice(r, r + FFN_ROW_CHUNK) for r in range(0, x_ref.shape[0], FFN_ROW_CHUNK)]
    acts = []
    for rows in chunks:
        h = h_scr[rows, :]
        gate = jnp.dot(h, wg_ref[...], preferred_element_type=F32)
        up = jnp.dot(h, wu_ref[...], preferred_element_type=F32)
        acts.append((gate * jax.nn.sigmoid(gate) * up).astype(BF16))
    for rows, a in zip(chunks, acts):
        y_ref[rows, :] += jnp.dot(a, wd_ref[...], preferred_element_type=F32)


def _ffn(x, g, w_gu, w_down, next_w=None, *, tm=1024, tf=512):
    nf = D_FF // tf
    steps = (SEQ // tm) * nf
    in_specs = [
        pl.BlockSpec((tm, D_MODEL), lambda i, j: (i, 0)),
        pl.BlockSpec((1, D_MODEL), lambda i, j: (0, 0)),
        pl.BlockSpec((D_MODEL, tf), lambda i, j: (0, j)),
        pl.BlockSpec((D_MODEL, tf), lambda i, j: (0, nf + j)),
        pl.BlockSpec((tf, D_MODEL), lambda i, j: (j, 0)),
    ]
    out_specs = [pl.BlockSpec((tm, D_MODEL), lambda i, j: (i, 0))]
    out_shape = [jax.ShapeDtypeStruct((SEQ, D_MODEL), F32)]
    args = [x, g, w_gu, w_gu, w_down]
    if next_w is not None:
        n_gu, n_down, layer = next_w
        gu_cols = 2 * D_FF // steps
        dn_rows = D_FF // steps
        assert gu_cols % LANES == 0 and dn_rows % 16 == 0
        in_specs += [
            pl.BlockSpec((None, D_MODEL, gu_cols), lambda i, j: (layer, 0, i * nf + j)),
            pl.BlockSpec((None, dn_rows, D_MODEL), lambda i, j: (layer, i * nf + j, 0)),
        ]
        out_specs += [
            pl.BlockSpec((D_MODEL, gu_cols), lambda i, j: (0, i * nf + j)),
            pl.BlockSpec((dn_rows, D_MODEL), lambda i, j: (i * nf + j, 0)),
        ]
        out_shape += [jax.ShapeDtypeStruct((D_MODEL, 2 * D_FF), BF16),
                      jax.ShapeDtypeStruct((D_FF, D_MODEL), BF16)]
        args += [n_gu, n_down]
    return _pipelined_call(
        functools.partial(_ffn_kernel, convert_next=next_w is not None),
        grid=(SEQ // tm, nf),
        in_specs=in_specs,
        out_specs=out_specs,
        out_shape=out_shape,
        scratch_shapes=[pltpu.VMEM((tm, D_MODEL), BF16)],
        name="ffn",
    )(*args)


def kernel(x, mix_norm_g, ffn_norm_g, fox_w_in, fox_b_f, fox_q_norm_g, fox_k_norm_g,
           fox_w_out, pool_w, pool_b, pool_scale, ffn_w_gate_up, ffn_w_down):
    assert x.shape == (1, SEQ, D_MODEL)
    x = x.reshape(SEQ, D_MODEL)
    tri = jnp.tril(jnp.ones((LANES, LANES), BF16))
    q_scale = HEAD_DIM ** -0.5 * LOG2E
    fox_w_in_t = jnp.swapaxes(fox_w_in, 1, 2)
    w_gu = w_down = None
    for i in range(DEPTH):
        j = i // 2
        g_mix = mix_norm_g[i][None, :]
        if i % 2 == 0:
            wf_t = jnp.pad(fox_w_in_t[j, 3 * D_MODEL:, :], ((0, LANES - N_HEADS), (0, 0)))
            bf = jnp.pad(fox_b_f[j], (0, LANES - N_HEADS))[None, :]
            gains = jnp.concatenate([
                jnp.tile(fox_q_norm_g[j] * q_scale, N_HEADS),
                jnp.tile(fox_k_norm_g[j], N_HEADS),
                jnp.ones((D_MODEL,), F32)])[None, :]
            qkv, ct = _fox_proj(x, g_mix, fox_w_in_t, j, gains, wf_t, bf, tri)
            qk_bound = (1.02 * HEAD_DIM * q_scale * jnp.max(jnp.abs(fox_q_norm_g[j]))
                        * jnp.max(jnp.abs(fox_k_norm_g[j])))
            par = jnp.stack([-(SKIP_EXP2 + 2.0 * qk_bound), qk_bound]).astype(F32)
            cend = jnp.pad(ct[:, ATTN_BLOCK - 1::ATTN_BLOCK],
                           ((0, 0), (0, LANES - SEQ // ATTN_BLOCK)))
            if i == 0:
                o, w_gu, w_down = _fox_attn(qkv, ct[:, None, :], cend[:, None, :], par,
                                            (ffn_w_gate_up, ffn_w_down, 0), t=ATTN_BLOCK)
            else:
                o, = _fox_attn(qkv, ct[:, None, :], cend[:, None, :], par, t=ATTN_BLOCK)
            x = _fox_out(x, o, fox_w_out, j)
        else:
            x = _pool_mix(x, g_mix, pool_w[j].astype(BF16), pool_b[j][None, :],
                          pool_scale[j][None, :])
        if i + 1 < DEPTH:
            x, w_gu, w_down = _ffn(x, ffn_norm_g[i][None, :], w_gu, w_down,
                                   (ffn_w_gate_up, ffn_w_down, i + 1))
        else:
            x, = _ffn(x, ffn_norm_g[i][None, :], w_gu, w_down)
    return x.reshape(1, SEQ, D_MODEL)
```

```python
import functools
import math

import jax
import jax.numpy as jnp
from jax import lax
from jax.experimental import pallas as pl
from jax.experimental.pallas import tpu as pltpu

D_MODEL = 2048
SEQ = 8192
DEPTH = 4
HEAD_DIM = 128
N_HEADS = D_MODEL // HEAD_DIM
POOL_WINDOWS = (2, 4, 8, 16)
POOL_GROUP = D_MODEL // len(POOL_WINDOWS)
POOL_HALO = 32
D_FF = 5632
RMS_EPS = 1e-6
NEG_INF = -1e30
LOG2E = math.log2(math.e)

LANES = 128
VMEM_LIMIT = 62 * 1024 * 1024

F32 = jnp.float32
BF16 = jnp.bfloat16


def _params(semantics):
    return pltpu.CompilerParams(dimension_semantics=semantics,
                                vmem_limit_bytes=VMEM_LIMIT)


def _rmsnorm(x, g):
    return x * lax.rsqrt(jnp.mean(x * x, axis=-1, keepdims=True) + RMS_EPS) * g


def _split3_bf16(x):
    x1 = x.astype(BF16)
    r1 = x - x1.astype(F32)
    x2 = r1.astype(BF16)
    x3 = (r1 - x2.astype(F32)).astype(BF16)
    return x1, x2, x3


ROW_CHUNK = 256
_NT_DIMS = (((1,), (1,)), ((), ()))


def _fox_proj_kernel(x_ref, g_ref, w_ref, gain_ref, wf_ref, bf_ref, tri_ref,
                     qkv_ref, ct_ref, h_scr, carry_scr, *, tm, tn):
    i = pl.program_id(0)
    j = pl.program_id(1)

    @pl.when(j == 0)
    def _():
        hb = _rmsnorm(x_ref[...], g_ref[...]).astype(BF16)
        h_scr[...] = hb
        f = lax.dot_general(hb, wf_ref[...].astype(BF16), _NT_DIMS,
                            preferred_element_type=F32) + bf_ref[...]
        logf = jnp.minimum(f, 0.0) - jnp.log1p(jnp.exp(-jnp.abs(f)))

        @pl.when(i == 0)
        def _():
            carry_scr[...] = jnp.zeros_like(carry_scr)

        tri = tri_ref[...]
        carry = carry_scr[0:1, :]
        chunks = []
        for r in range(tm // LANES):
            x1, x2, x3 = _split3_bf16(logf[r * LANES:(r + 1) * LANES, :])
            cs = (jnp.dot(tri, x1, preferred_element_type=F32)
                  + jnp.dot(tri, x2, preferred_element_type=F32)
                  + jnp.dot(tri, x3, preferred_element_type=F32)) + carry
            carry = cs[LANES - 1:LANES, :]
            chunks.append(cs)
        carry_scr[0:1, :] = carry
        c = jnp.concatenate(chunks, axis=0) * LOG2E
        ct_ref[...] = c.T[:N_HEADS, :]

    w_t = w_ref[...].astype(BF16)
    is_qk = (j < 2 * D_MODEL // tn).astype(F32)
    for rc in range(tm // ROW_CHUNK):
        rows = slice(rc * ROW_CHUNK, (rc + 1) * ROW_CHUNK)
        r = lax.dot_general(h_scr[rows, :], w_t, _NT_DIMS,
                            preferred_element_type=F32)
        for hh in range(tn // HEAD_DIM):
            sl = slice(hh * HEAD_DIM, (hh + 1) * HEAD_DIM)
            rh = r[:, sl]
            inv = lax.rsqrt(jnp.mean(rh * rh, axis=-1, keepdims=True) + RMS_EPS)
            scale = is_qk * inv + (1.0 - is_qk)
            qkv_ref[rows, sl] = (rh * scale * gain_ref[:, sl]).astype(BF16)


def _fox_proj(x, g, w_in_t, layer, gains, wf_t, bf, tri, *, tm=1024, tn=1024):
    n = 3 * D_MODEL
    return pl.pallas_call(
        functools.partial(_fox_proj_kernel, tm=tm, tn=tn),
        grid=(SEQ // tm, n // tn),
        in_specs=[
            pl.BlockSpec((tm, D_MODEL), lambda i, j: (i, 0)),
            pl.BlockSpec((1, D_MODEL), lambda i, j: (0, 0)),
            pl.BlockSpec((None, tn, D_MODEL), lambda i, j: (layer, j, 0)),
            pl.BlockSpec((1, tn), lambda i, j: (0, j)),
            pl.BlockSpec((LANES, D_MODEL), lambda i, j: (0, 0)),
            pl.BlockSpec((1, LANES), lambda i, j: (0, 0)),
            pl.BlockSpec((LANES, LANES), lambda i, j: (0, 0)),
        ],
        out_specs=[
            pl.BlockSpec((tm, tn), lambda i, j: (i, j)),
            pl.BlockSpec((N_HEADS, tm), lambda i, j: (0, i)),
        ],
        out_shape=[
            jax.ShapeDtypeStruct((SEQ, n), BF16),
            jax.ShapeDtypeStruct((N_HEADS, SEQ), F32),
        ],
        scratch_shapes=[
            pltpu.VMEM((tm, D_MODEL), BF16),
            pltpu.VMEM((8, LANES), F32),
        ],
        compiler_params=_params(("arbitrary", "arbitrary")),
        name="fox_proj",
    )(x, g, w_in_t, gains, wf_t, bf, tri)


SKIP_EXP2 = 160.0
ATTN_BLOCK = 512
MAX_FIXED_SHIFT_BOUND = 48.0


def _fox_attn_kernel(*refs, t, convert):
    if convert:
        (par_ref, q_ref, k_ref, v_ref, c_ref, cend_ref, wa_ref, wb_ref,
         o_ref, wa_out, wb_out, m_scr, l_scr, acc_scr) = refs
        wa_out[...] = wa_ref[...].astype(BF16)
        wb_out[...] = wb_ref[...].astype(BF16)
    else:
        (par_ref, q_ref, k_ref, v_ref, c_ref, cend_ref,
         o_ref, m_scr, l_scr, acc_scr) = refs
    blk = lax.broadcasted_iota(jnp.int32, (1, LANES), 1)
    cend = cend_ref[0]
    thr = par_ref[0]
    qk_bound = par_ref[1]

    def causal(x):
        row = lax.broadcasted_iota(jnp.int32, (t, t), 0)
        col = lax.broadcasted_iota(jnp.int32, (t, t), 1)
        return jnp.where(row >= col, x, NEG_INF)

    def scores(q, c_q0, j):
        k0 = pl.multiple_of(j * t, t)
        s = lax.dot_general(q, k_ref[pl.ds(k0, t), :], _NT_DIMS,
                            preferred_element_type=F32)
        return s + (c_q0 - c_ref[0, :, pl.ds(k0, t)])

    def v_block(j):
        return v_ref[pl.ds(pl.multiple_of(j * t, t), t), :]

    def q_block(qi, fixed_shift):
        q0 = pl.multiple_of(qi * t, t)
        q = q_ref[pl.ds(q0, t), :]
        c_q = c_ref[0, :, pl.ds(q0, t)]
        c_q0 = c_q[:, 0:1]
        skippable = ((c_q0 - cend) < thr) & (blk < qi)
        first = jnp.sum(skippable.astype(jnp.int32))
        l_scr[...] = jnp.zeros_like(l_scr)
        acc_scr[...] = jnp.zeros_like(acc_scr)

        if fixed_shift:
            shift = qk_bound + (c_q0 - c_q)
            shift = jnp.broadcast_to(shift, (LANES, t)).T
            shift = jnp.concatenate([shift] * (t // LANES), axis=1)

            def step(j, masked):
                x = scores(q, c_q0, j) - shift
                p = jnp.exp2(causal(x) if masked else x)
                part = p[:, 0:LANES]
                for n in range(1, t // LANES):
                    part = part + p[:, n * LANES:(n + 1) * LANES]
                l_scr[...] += part
                acc_scr[...] += jnp.dot(p.astype(BF16), v_block(j),
                                        preferred_element_type=F32)
        else:
            m_scr[...] = jnp.full_like(m_scr, NEG_INF)

            def step(j, masked):
                s = scores(q, c_q0, j)
                if masked:
                    s = causal(s)
                m_prev = m_scr[...]
                m_next = jnp.maximum(m_prev, jnp.max(s, axis=1, keepdims=True))
                p = jnp.exp2(s - jnp.concatenate([m_next] * (t // LANES), axis=1))
                alpha = jnp.exp2(m_prev - m_next)
                l_scr[...] = alpha * l_scr[...] + jnp.sum(p, axis=1, keepdims=True)
                m_scr[...] = m_next
                acc_scr[...] = alpha * acc_scr[...] + jnp.dot(
                    p.astype(BF16), v_block(j), preferred_element_type=F32)

        step(qi, True)

        n_pairs = (qi - first) // 2

        def kv_pair(n, c):
            j = first + 2 * n
            step(j, False)
            step(j + 1, False)
            return c

        lax.fori_loop(0, n_pairs, kv_pair, 0)

        @pl.when(first + 2 * n_pairs < qi)
        def _():
            step(qi - 1, False)

        l = l_scr[...]
        if fixed_shift:
            l = jnp.sum(l, axis=1, keepdims=True)
        o_ref[pl.ds(q0, t), :] = (acc_scr[...] / l).astype(BF16)

    def all_q_blocks(fixed_shift):
        def body(qi, carry):
            q_block(qi, fixed_shift)
            return carry
        lax.fori_loop(0, SEQ // t, body, 0)

    @pl.when(qk_bound < MAX_FIXED_SHIFT_BOUND)
    def _():
        all_q_blocks(True)

    @pl.when(qk_bound >= MAX_FIXED_SHIFT_BOUND)
    def _():
        all_q_blocks(False)


def _fox_attn(qkv, c3, cend3, par, cast_w=None, *, t=512):
    in_specs = [
        pl.BlockSpec(memory_space=pltpu.SMEM),
        pl.BlockSpec((SEQ, HEAD_DIM), lambda h: (0, h)),
        pl.BlockSpec((SEQ, HEAD_DIM), lambda h: (0, N_HEADS + h)),
        pl.BlockSpec((SEQ, HEAD_DIM), lambda h: (0, 2 * N_HEADS + h)),
        pl.BlockSpec((1, 1, SEQ), lambda h: (h, 0, 0)),
        pl.BlockSpec((1, 1, LANES), lambda h: (h, 0, 0)),
    ]
    out_specs = [pl.BlockSpec((SEQ, HEAD_DIM), lambda h: (0, h))]
    out_shape = [jax.ShapeDtypeStruct((SEQ, D_MODEL), BF16)]
    args = [par, qkv, qkv, qkv, c3, cend3]
    if cast_w is not None:
        wa, wb, layer = cast_w
        for w in (wa, wb):
            rows, cols = w.shape[1] // N_HEADS, w.shape[2]
            assert rows % 16 == 0
            in_specs.append(pl.BlockSpec((None, rows, cols), lambda h: (layer, h, 0)))
            out_specs.append(pl.BlockSpec((rows, cols), lambda h: (h, 0)))
            out_shape.append(jax.ShapeDtypeStruct(w.shape[1:], BF16))
        args += [wa, wb]
    return pl.pallas_call(
        functools.partial(_fox_attn_kernel, t=t, convert=cast_w is not None),
        grid=(N_HEADS,),
        in_specs=in_specs,
        out_specs=out_specs,
        out_shape=out_shape,
        scratch_shapes=[
            pltpu.VMEM((t, LANES), F32),
            pltpu.VMEM((t, LANES), F32),
            pltpu.VMEM((t, HEAD_DIM), F32),
        ],
        compiler_params=_params(("arbitrary",)),
        name="fox_attn",
    )(*args)


def _fox_out_kernel(x_ref, o_ref, w_ref, y_ref, wb_scr):
    @pl.when(pl.program_id(0) == 0)
    def _():
        wb_scr[...] = w_ref[...].astype(BF16)

    y_ref[...] = x_ref[...] + jnp.dot(o_ref[...], wb_scr[...],
                                      preferred_element_type=F32)


def _fox_out(x, o, w_out, layer, *, tm=512):
    return pl.pallas_call(
        _fox_out_kernel,
        grid=(SEQ // tm,),
        in_specs=[
            pl.BlockSpec((tm, D_MODEL), lambda i: (i, 0)),
            pl.BlockSpec((tm, D_MODEL), lambda i: (i, 0)),
            pl.BlockSpec((None, D_MODEL, D_MODEL), lambda i: (layer, 0, 0),
                         pipeline_mode=pl.Buffered(1)),
        ],
        out_specs=pl.BlockSpec((tm, D_MODEL), lambda i: (i, 0)),
        out_shape=jax.ShapeDtypeStruct((SEQ, D_MODEL), F32),
        scratch_shapes=[pltpu.VMEM((D_MODEL, D_MODEL), BF16)],
        compiler_params=_params(("arbitrary",)),
        name="fox_out",
    )(x, o, w_out)


def _pool_kernel(x_ref, g_ref, w_ref, b_ref, s_ref, y_ref, h_scr, a_scr, b_scr, *, tm):
    i = pl.program_id(0)
    end = POOL_HALO + tm

    @pl.when(i == 0)
    def _():
        h_scr[0:POOL_HALO, :] = jnp.zeros((POOL_HALO, D_MODEL), F32)

    x = x_ref[...]
    h_scr[POOL_HALO:end, :] = _rmsnorm(x, g_ref[...])
    t = i * tm + lax.broadcasted_iota(jnp.int32, (tm, 1), 0)
    for gi, win in enumerate(POOL_WINDOWS):
        cols = slice(gi * POOL_GROUP, (gi + 1) * POOL_GROUP)
        h = h_scr[POOL_HALO:end, cols]
        levels = win.bit_length() - 1
        src = lambda lo, hi: h_scr[lo:hi, cols]
        for k in range(1, levels + 1):
            lo = POOL_HALO - 8 * (levels - k)
            shift = 2 ** (k - 1)
            tot = src(lo, end) + src(lo - shift, end - shift)
            if k < levels:
                buf = a_scr if k % 2 else b_scr
                buf[lo:end, :] = tot
                src = lambda lo, hi, buf=buf: buf[lo:hi, :]
        cnt = jnp.minimum(t + 1, win).astype(F32)
        y = (tot / cnt - h).astype(BF16)
        y = jnp.dot(y, w_ref[gi], preferred_element_type=F32)
        y_ref[:, cols] = x[:, cols] + (y + b_ref[:, cols]) * s_ref[:, cols]
    h_scr[0:POOL_HALO, :] = h_scr[tm:tm + POOL_HALO, :]


def _pool_mix(x, g, w, b, scale, *, tm=512):
    n_groups = len(POOL_WINDOWS)
    return pl.pallas_call(
        functools.partial(_pool_kernel, tm=tm),
        grid=(SEQ // tm,),
        in_specs=[
            pl.BlockSpec((tm, D_MODEL), lambda i: (i, 0)),
            pl.BlockSpec((1, D_MODEL), lambda i: (0, 0)),
            pl.BlockSpec((n_groups, POOL_GROUP, POOL_GROUP), lambda i: (0, 0, 0)),
            pl.BlockSpec((1, D_MODEL), lambda i: (0, 0)),
            pl.BlockSpec((1, D_MODEL), lambda i: (0, 0)),
        ],
        out_specs=pl.BlockSpec((tm, D_MODEL), lambda i: (i, 0)),
        out_shape=jax.ShapeDtypeStruct((SEQ, D_MODEL), F32),
        scratch_shapes=[pltpu.VMEM((POOL_HALO + tm, D_MODEL), F32),
                        pltpu.VMEM((POOL_HALO + tm, POOL_GROUP), F32),
                        pltpu.VMEM((POOL_HALO + tm, POOL_GROUP), F32)],
        compiler_params=_params(("arbitrary",)),
        name="pool_mix",
    )(x, g, w, b, scale)


FFN_ROW_CHUNK = 256


def _ffn_kernel(*refs, convert_next):
    if convert_next:
        (x_ref, g_ref, wg_ref, wu_ref, wd_ref, ngu_ref, ndn_ref,
         y_ref, ngu_out, ndn_out, h_scr) = refs
        ngu_out[...] = ngu_ref[...].astype(BF16)
        ndn_out[...] = ndn_ref[...].astype(BF16)
    else:
        x_ref, g_ref, wg_ref, wu_ref, wd_ref, y_ref, h_scr = refs
    j = pl.program_id(1)

    chunks = [slice(r, r + FFN_ROW_CHUNK) for r in range(0, x_ref.shape[0], FFN_ROW_CHUNK)]

    def tile(h_rows):
        acts = []
        for rows in chunks:
            h = h_rows(rows)
            gate = jnp.dot(h, wg_ref[...], preferred_element_type=F32)
            up = jnp.dot(h, wu_ref[...], preferred_element_type=F32)
            acts.append((gate * jax.nn.sigmoid(gate) * up).astype(BF16))
        for rows, a in zip(chunks, acts):
            y_ref[rows, :] += jnp.dot(a, wd_ref[...], preferred_element_type=F32)

    @pl.when(j == 0)
    def _():
        def norm_rows(rows):
            x = x_ref[rows, :]
            h = _rmsnorm(x, g_ref[...]).astype(BF16)
            h_scr[rows, :] = h
            y_ref[rows, :] = x
            return h
        tile(norm_rows)

    @pl.when(j > 0)
    def _():
        tile(lambda rows: h_scr[rows, :])


def _ffn(x, g, w_gu, w_down, next_w=None, *, tm=1024, tf=512):
    nf = D_FF // tf
    steps = (SEQ // tm) * nf
    in_specs = [
        pl.BlockSpec((tm, D_MODEL), lambda i, j: (i, 0)),
        pl.BlockSpec((1, D_MODEL), lambda i, j: (0, 0)),
        pl.BlockSpec((D_MODEL, tf), lambda i, j: (0, j)),
        pl.BlockSpec((D_MODEL, tf), lambda i, j: (0, nf + j)),
        pl.BlockSpec((tf, D_MODEL), lambda i, j: (j, 0)),
    ]
    out_specs = [pl.BlockSpec((tm, D_MODEL), lambda i, j: (i, 0))]
    out_shape = [jax.ShapeDtypeStruct((SEQ, D_MODEL), F32)]
    args = [x, g, w_gu, w_gu, w_down]
    if next_w is not None:
        n_gu, n_down, layer = next_w
        gu_cols = 2 * D_FF // steps
        dn_rows = D_FF // steps
        assert gu_cols % LANES == 0 and dn_rows % 16 == 0
        in_specs += [
            pl.BlockSpec((None, D_MODEL, gu_cols), lambda i, j: (layer, 0, i * nf + j)),
            pl.BlockSpec((None, dn_rows, D_MODEL), lambda i, j: (layer, i * nf + j, 0)),
        ]
        out_specs += [
            pl.BlockSpec((D_MODEL, gu_cols), lambda i, j: (0, i * nf + j)),
            pl.BlockSpec((dn_rows, D_MODEL), lambda i, j: (i * nf + j, 0)),
        ]
        out_shape += [jax.ShapeDtypeStruct((D_MODEL, 2 * D_FF), BF16),
                      jax.ShapeDtypeStruct((D_FF, D_MODEL), BF16)]
        args += [n_gu, n_down]
    return pl.pallas_call(
        functools.partial(_ffn_kernel, convert_next=next_w is not None),
        grid=(SEQ // tm, nf),
        in_specs=in_specs,
        out_specs=out_specs,
        out_shape=out_shape,
        scratch_shapes=[pltpu.VMEM((tm, D_MODEL), BF16)],
        compiler_params=_params(("arbitrary", "arbitrary")),
        name="ffn",
    )(*args)


def kernel(x, mix_norm_g, ffn_norm_g, fox_w_in, fox_b_f, fox_q_norm_g, fox_k_norm_g,
           fox_w_out, pool_w, pool_b, pool_scale, ffn_w_gate_up, ffn_w_down):
    assert x.shape == (1, SEQ, D_MODEL)
    x = x.reshape(SEQ, D_MODEL)
    tri = jnp.tril(jnp.ones((LANES, LANES), BF16))
    q_scale = HEAD_DIM ** -0.5 * LOG2E
    fox_w_in_t = jnp.swapaxes(fox_w_in, 1, 2)
    w_gu = w_down = None
    for i in range(DEPTH):
        j = i // 2
        g_mix = mix_norm_g[i][None, :]
        if i % 2 == 0:
            wf_t = jnp.pad(fox_w_in_t[j, 3 * D_MODEL:, :], ((0, LANES - N_HEADS), (0, 0)))
            bf = jnp.pad(fox_b_f[j], (0, LANES - N_HEADS))[None, :]
            gains = jnp.concatenate([
                jnp.tile(fox_q_norm_g[j] * q_scale, N_HEADS),
                jnp.tile(fox_k_norm_g[j], N_HEADS),
                jnp.ones((D_MODEL,), F32)])[None, :]
            qkv, ct = _fox_proj(x, g_mix, fox_w_in_t, j, gains, wf_t, bf, tri)
            qk_bound = (1.02 * HEAD_DIM * q_scale * jnp.max(jnp.abs(fox_q_norm_g[j]))
                        * jnp.max(jnp.abs(fox_k_norm_g[j])))
            par = jnp.stack([-(SKIP_EXP2 + 2.0 * qk_bound), qk_bound]).astype(F32)
            cend = jnp.pad(ct[:, ATTN_BLOCK - 1::ATTN_BLOCK],
                           ((0, 0), (0, LANES - SEQ // ATTN_BLOCK)))
            if i == 0:
                o, w_gu, w_down = _fox_attn(qkv, ct[:, None, :], cend[:, None, :], par,
                                            (ffn_w_gate_up, ffn_w_down, 0), t=ATTN_BLOCK)
            else:
                o, = _fox_attn(qkv, ct[:, None, :], cend[:, None, :], par, t=ATTN_BLOCK)
            x = _fox_out(x, o, fox_w_out, j)
        else:
            x = _pool_mix(x, g_mix, pool_w[j].astype(BF16), pool_b[j][None, :],
                          pool_scale[j][None, :])
        if i + 1 < DEPTH:
            x, w_gu, w_down = _ffn(x, ffn_norm_g[i][None, :], w_gu, w_down,
                                   (ffn_w_gate_up, ffn_w_down, i + 1))
        else:
            x, = _ffn(x, ffn_norm_g[i][None, :], w_gu, w_down)
    return x.reshape(1, SEQ, D_MODEL)
```

```python
import functools
import math

import jax
import jax.numpy as jnp
from jax import lax
from jax.experimental import pallas as pl
from jax.experimental.pallas import tpu as pltpu

D_MODEL = 2048
SEQ = 8192
DEPTH = 4
HEAD_DIM = 128
N_HEADS = D_MODEL // HEAD_DIM
POOL_WINDOWS = (2, 4, 8, 16)
POOL_GROUP = D_MODEL // len(POOL_WINDOWS)
POOL_HALO = 32
D_FF = 5632
RMS_EPS = 1e-6
NEG_INF = -1e30
LOG2E = math.log2(math.e)

LANES = 128
VMEM_LIMIT = 62 * 1024 * 1024

F32 = jnp.float32
BF16 = jnp.bfloat16


def _params(semantics):
    return pltpu.CompilerParams(dimension_semantics=semantics,
                                vmem_limit_bytes=VMEM_LIMIT)


def _rmsnorm(x, g):
    return x * lax.rsqrt(jnp.mean(x * x, axis=-1, keepdims=True) + RMS_EPS) * g


def _split3_bf16(x):
    x1 = x.astype(BF16)
    r1 = x - x1.astype(F32)
    x2 = r1.astype(BF16)
    x3 = (r1 - x2.astype(F32)).astype(BF16)
    return x1, x2, x3


ROW_CHUNK = 256
_NT_DIMS = (((1,), (1,)), ((), ()))


def _fox_proj_kernel(x_ref, g_ref, w_ref, gain_ref, wf_ref, bf_ref, tri_ref,
                     qkv_ref, ct_ref, h_scr, carry_scr, *, tm, tn):
    i = pl.program_id(0)
    j = pl.program_id(1)

    @pl.when(j == 0)
    def _():
        hb = _rmsnorm(x_ref[...], g_ref[...]).astype(BF16)
        h_scr[...] = hb
        f = lax.dot_general(hb, wf_ref[...].astype(BF16), _NT_DIMS,
                            preferred_element_type=F32) + bf_ref[...]
        logf = jnp.minimum(f, 0.0) - jnp.log1p(jnp.exp(-jnp.abs(f)))

        @pl.when(i == 0)
        def _():
            carry_scr[...] = jnp.zeros_like(carry_scr)

        tri = tri_ref[...]
        carry = carry_scr[0:1, :]
        chunks = []
        for r in range(tm // LANES):
            x1, x2, x3 = _split3_bf16(logf[r * LANES:(r + 1) * LANES, :])
            cs = (jnp.dot(tri, x1, preferred_element_type=F32)
                  + jnp.dot(tri, x2, preferred_element_type=F32)
                  + jnp.dot(tri, x3, preferred_element_type=F32)) + carry
            carry = cs[LANES - 1:LANES, :]
            chunks.append(cs)
        carry_scr[0:1, :] = carry
        c = jnp.concatenate(chunks, axis=0) * LOG2E
        ct_ref[...] = c.T[:N_HEADS, :]

    w_t = w_ref[...].astype(BF16)
    is_qk = (j < 2 * D_MODEL // tn).astype(F32)
    for rc in range(tm // ROW_CHUNK):
        rows = slice(rc * ROW_CHUNK, (rc + 1) * ROW_CHUNK)
        r = lax.dot_general(h_scr[rows, :], w_t, _NT_DIMS,
                            preferred_element_type=F32)
        for hh in range(tn // HEAD_DIM):
            sl = slice(hh * HEAD_DIM, (hh + 1) * HEAD_DIM)
            rh = r[:, sl]
            inv = lax.rsqrt(jnp.mean(rh * rh, axis=-1, keepdims=True) + RMS_EPS)
            scale = is_qk * inv + (1.0 - is_qk)
            qkv_ref[rows, sl] = (rh * scale * gain_ref[:, sl]).astype(BF16)


def _fox_proj(x, g, w_in_t, layer, gains, wf_t, bf, tri, *, tm=1024, tn=1024):
    n = 3 * D_MODEL
    return pl.pallas_call(
        functools.partial(_fox_proj_kernel, tm=tm, tn=tn),
        grid=(SEQ // tm, n // tn),
        in_specs=[
            pl.BlockSpec((tm, D_MODEL), lambda i, j: (i, 0)),
            pl.BlockSpec((1, D_MODEL), lambda i, j: (0, 0)),
            pl.BlockSpec((None, tn, D_MODEL), lambda i, j: (layer, j, 0)),
            pl.BlockSpec((1, tn), lambda i, j: (0, j)),
            pl.BlockSpec((LANES, D_MODEL), lambda i, j: (0, 0)),
            pl.BlockSpec((1, LANES), lambda i, j: (0, 0)),
            pl.BlockSpec((LANES, LANES), lambda i, j: (0, 0)),
        ],
        out_specs=[
            pl.BlockSpec((tm, tn), lambda i, j: (i, j)),
            pl.BlockSpec((N_HEADS, tm), lambda i, j: (0, i)),
        ],
        out_shape=[
            jax.ShapeDtypeStruct((SEQ, n), BF16),
            jax.ShapeDtypeStruct((N_HEADS, SEQ), F32),
        ],
        scratch_shapes=[
            pltpu.VMEM((tm, D_MODEL), BF16),
            pltpu.VMEM((8, LANES), F32),
        ],
        compiler_params=_params(("arbitrary", "arbitrary")),
        name="fox_proj",
    )(x, g, w_in_t, gains, wf_t, bf, tri)


SKIP_EXP2 = 151.0
ATTN_BLOCK = 512
MAX_FIXED_SHIFT_BOUND = 48.0


def _fox_attn_kernel(*refs, t, convert):
    if convert:
        (par_ref, q_ref, k_ref, v_ref, c_ref, cend_ref, wa_ref, wb_ref,
         o_ref, wa_out, wb_out, m_scr, l_scr, acc_scr) = refs
        wa_out[...] = wa_ref[...].astype(BF16)
        wb_out[...] = wb_ref[...].astype(BF16)
    else:
        (par_ref, q_ref, k_ref, v_ref, c_ref, cend_ref,
         o_ref, m_scr, l_scr, acc_scr) = refs
    blk = lax.broadcasted_iota(jnp.int32, (1, LANES), 1)
    cend = cend_ref[0]
    thr = par_ref[0]
    qk_bound = par_ref[1]

    def causal(x):
        row = lax.broadcasted_iota(jnp.int32, (t, t), 0)
        col = lax.broadcasted_iota(jnp.int32, (t, t), 1)
        return jnp.where(row >= col, x, NEG_INF)

    def scores(q, c_q0, j):
        k0 = pl.multiple_of(j * t, t)
        s = lax.dot_general(q, k_ref[pl.ds(k0, t), :], _NT_DIMS,
                            preferred_element_type=F32)
        return s + (c_q0 - c_ref[0, :, pl.ds(k0, t)])

    def v_block(j):
        return v_ref[pl.ds(pl.multiple_of(j * t, t), t), :]

    def q_block(qi, fixed_shift):
        q0 = pl.multiple_of(qi * t, t)
        q = q_ref[pl.ds(q0, t), :]
        c_q = c_ref[0, :, pl.ds(q0, t)]
        c_q0 = c_q[:, 0:1]
        skippable = ((c_q0 - cend) < thr) & (blk < qi)
        first = jnp.sum(skippable.astype(jnp.int32))
        l_scr[...] = jnp.zeros_like(l_scr)
        acc_scr[...] = jnp.zeros_like(acc_scr)

        if fixed_shift:
            shift = qk_bound + (c_q0 - c_q)
            shift = jnp.broadcast_to(shift, (LANES, t)).T
            shift = jnp.concatenate([shift] * (t // LANES), axis=1)

            def step(j, masked):
                x = scores(q, c_q0, j) - shift
                p = jnp.exp2(causal(x) if masked else x)
                part = p[:, 0:LANES]
                for n in range(1, t // LANES):
                    part = part + p[:, n * LANES:(n + 1) * LANES]
                l_scr[...] += part
                acc_scr[...] += jnp.dot(p.astype(BF16), v_block(j),
                                        preferred_element_type=F32)
        else:
            m_scr[...] = jnp.full_like(m_scr, NEG_INF)

            def step(j, masked):
                s = scores(q, c_q0, j)
                if masked:
                    s = causal(s)
                m_prev = m_scr[...]
                m_next = jnp.maximum(m_prev, jnp.max(s, axis=1, keepdims=True))
                p = jnp.exp2(s - jnp.concatenate([m_next] * (t // LANES), axis=1))
                alpha = jnp.exp2(m_prev - m_next)
                l_scr[...] = alpha * l_scr[...] + jnp.sum(p, axis=1, keepdims=True)
                m_scr[...] = m_next
                acc_scr[...] = alpha * acc_scr[...] + jnp.dot(
                    p.astype(BF16), v_block(j), preferred_element_type=F32)

        step(qi, True)

        n_pairs = (qi - first) // 2

        def kv_pair(n, c):
            j = first + 2 * n
            step(j, False)
            step(j + 1, False)
            return c

        lax.fori_loop(0, n_pairs, kv_pair, 0)

        @pl.when(first + 2 * n_pairs < qi)
        def _():
            step(qi - 1, False)

        l = l_scr[...]
        if fixed_shift:
            l = jnp.sum(l, axis=1, keepdims=True)
        o_ref[pl.ds(q0, t), :] = (acc_scr[...] / l).astype(BF16)

    def all_q_blocks(fixed_shift):
        def body(qi, carry):
            q_block(qi, fixed_shift)
            return carry
        lax.fori_loop(0, SEQ // t, body, 0)

    @pl.when(qk_bound < MAX_FIXED_SHIFT_BOUND)
    def _():
        all_q_blocks(True)

    @pl.when(qk_bound >= MAX_FIXED_SHIFT_BOUND)
    def _():
        all_q_blocks(False)


def _fox_attn(qkv, c3, cend3, par, cast_w=None, *, t=512):
    in_specs = [
        pl.BlockSpec(memory_space=pltpu.SMEM),
        pl.BlockSpec((SEQ, HEAD_DIM), lambda h: (0, h)),
        pl.BlockSpec((SEQ, HEAD_DIM), lambda h: (0, N_HEADS + h)),
        pl.BlockSpec((SEQ, HEAD_DIM), lambda h: (0, 2 * N_HEADS + h)),
        pl.BlockSpec((1, 1, SEQ), lambda h: (h, 0, 0)),
        pl.BlockSpec((1, 1, LANES), lambda h: (h, 0, 0)),
    ]
    out_specs = [pl.BlockSpec((SEQ, HEAD_DIM), lambda h: (0, h))]
    out_shape = [jax.ShapeDtypeStruct((SEQ, D_MODEL), BF16)]
    args = [par, qkv, qkv, qkv, c3, cend3]
    if cast_w is not None:
        wa, wb, layer = cast_w
        for w in (wa, wb):
            rows, cols = w.shape[1] // N_HEADS, w.shape[2]
            assert rows % 16 == 0
            in_specs.append(pl.BlockSpec((None, rows, cols), lambda h: (layer, h, 0)))
            out_specs.append(pl.BlockSpec((rows, cols), lambda h: (h, 0)))
            out_shape.append(jax.ShapeDtypeStruct(w.shape[1:], BF16))
        args += [wa, wb]
    return pl.pallas_call(
        functools.partial(_fox_attn_kernel, t=t, convert=cast_w is not None),
        grid=(N_HEADS,),
        in_specs=in_specs,
        out_specs=out_specs,
        out_shape=out_shape,
        scratch_shapes=[
            pltpu.VMEM((t, LANES), F32),
            pltpu.VMEM((t, LANES), F32),
            pltpu.VMEM((t, HEAD_DIM), F32),
        ],
        compiler_params=_params(("arbitrary",)),
        name="fox_attn",
    )(*args)


def _fox_out_kernel(x_ref, o_ref, w_ref, y_ref, wb_scr):
    @pl.when(pl.program_id(0) == 0)
    def _():
        wb_scr[...] = w_ref[...].astype(BF16)

    y_ref[...] = x_ref[...] + jnp.dot(o_ref[...], wb_scr[...],
                                      preferred_element_type=F32)


def _fox_out(x, o, w_out, layer, *, tm=512):
    return pl.pallas_call(
        _fox_out_kernel,
        grid=(SEQ // tm,),
        in_specs=[
            pl.BlockSpec((tm, D_MODEL), lambda i: (i, 0)),
            pl.BlockSpec((tm, D_MODEL), lambda i: (i, 0)),
            pl.BlockSpec((None, D_MODEL, D_MODEL), lambda i: (layer, 0, 0),
                         pipeline_mode=pl.Buffered(1)),
        ],
        out_specs=pl.BlockSpec((tm, D_MODEL), lambda i: (i, 0)),
        out_shape=jax.ShapeDtypeStruct((SEQ, D_MODEL), F32),
        scratch_shapes=[pltpu.VMEM((D_MODEL, D_MODEL), BF16)],
        compiler_params=_params(("arbitrary",)),
        name="fox_out",
    )(x, o, w_out)


def _pool_kernel(x_ref, g_ref, w_ref, b_ref, s_ref, y_ref, h_scr, a_scr, b_scr, wb_scr,
                 *, tm):
    i = pl.program_id(0)
    end = POOL_HALO + tm

    @pl.when(i == 0)
    def _():
        h_scr[0:POOL_HALO, :] = jnp.zeros((POOL_HALO, D_MODEL), F32)
        wb_scr[...] = w_ref[...].astype(BF16)

    x = x_ref[...]
    h_scr[POOL_HALO:end, :] = _rmsnorm(x, g_ref[...])
    t = i * tm + lax.broadcasted_iota(jnp.int32, (tm, 1), 0)
    for gi, win in enumerate(POOL_WINDOWS):
        cols = slice(gi * POOL_GROUP, (gi + 1) * POOL_GROUP)
        h = h_scr[POOL_HALO:end, cols]
        levels = win.bit_length() - 1
        src = lambda lo, hi: h_scr[lo:hi, cols]
        for k in range(1, levels + 1):
            lo = POOL_HALO - 8 * (levels - k)
            shift = 2 ** (k - 1)
            tot = src(lo, end) + src(lo - shift, end - shift)
            if k < levels:
                buf = a_scr if k % 2 else b_scr
                buf[lo:end, :] = tot
                src = lambda lo, hi, buf=buf: buf[lo:hi, :]
        cnt = jnp.minimum(t + 1, win).astype(F32)
        y = (tot / cnt - h).astype(BF16)
        y = jnp.dot(y, wb_scr[gi], preferred_element_type=F32)
        y_ref[:, cols] = x[:, cols] + (y + b_ref[:, cols]) * s_ref[:, cols]
    h_scr[0:POOL_HALO, :] = h_scr[tm:tm + POOL_HALO, :]


def _pool_mix(x, g, w, layer, b, scale, *, tm=512):
    n_groups = len(POOL_WINDOWS)
    return pl.pallas_call(
        functools.partial(_pool_kernel, tm=tm),
        grid=(SEQ // tm,),
        in_specs=[
            pl.BlockSpec((tm, D_MODEL), lambda i: (i, 0)),
            pl.BlockSpec((1, D_MODEL), lambda i: (0, 0)),
            pl.BlockSpec((None, n_groups, POOL_GROUP, POOL_GROUP),
                         lambda i: (layer, 0, 0, 0), pipeline_mode=pl.Buffered(1)),
            pl.BlockSpec((1, D_MODEL), lambda i: (0, 0)),
            pl.BlockSpec((1, D_MODEL), lambda i: (0, 0)),
        ],
        out_specs=pl.BlockSpec((tm, D_MODEL), lambda i: (i, 0)),
        out_shape=jax.ShapeDtypeStruct((SEQ, D_MODEL), F32),
        scratch_shapes=[pltpu.VMEM((POOL_HALO + tm, D_MODEL), F32),
                        pltpu.VMEM((POOL_HALO + tm, POOL_GROUP), F32),
                        pltpu.VMEM((POOL_HALO + tm, POOL_GROUP), F32),
                        pltpu.VMEM((n_groups, POOL_GROUP, POOL_GROUP), BF16)],
        compiler_params=_params(("arbitrary",)),
        name="pool_mix",
    )(x, g, w, b, scale)


FFN_ROW_CHUNK = 256


def _ffn_kernel(*refs, convert_next):
    if convert_next:
        (x_ref, g_ref, wg_ref, wu_ref, wd_ref, ngu_ref, ndn_ref,
         y_ref, ngu_out, ndn_out, h_scr) = refs
        ngu_out[...] = ngu_ref[...].astype(BF16)
        ndn_out[...] = ndn_ref[...].astype(BF16)
    else:
        x_ref, g_ref, wg_ref, wu_ref, wd_ref, y_ref, h_scr = refs
    j = pl.program_id(1)

    @pl.when(j == 0)
    def _():
        x = x_ref[...]
        h_scr[...] = _rmsnorm(x, g_ref[...]).astype(BF16)
        y_ref[...] = x

    chunks = [slice(r, r + FFN_ROW_CHUNK) for r in range(0, x_ref.shape[0], FFN_ROW_CHUNK)]
    acts = []
    for rows in chunks:
        h = h_scr[rows, :]
        gate = jnp.dot(h, wg_ref[...], preferred_element_type=F32)
        up = jnp.dot(h, wu_ref[...], preferred_element_type=F32)
        acts.append((gate * jax.nn.sigmoid(gate) * up).astype(BF16))
    for rows, a in zip(chunks, acts):
        y_ref[rows, :] += jnp.dot(a, wd_ref[...], preferred_element_type=F32)


def _ffn(x, g, w_gu, w_down, next_w=None, *, tm=1024, tf=512):
    nf = D_FF // tf
    steps = (SEQ // tm) * nf
    in_specs = [
        pl.BlockSpec((tm, D_MODEL), lambda i, j: (i, 0)),
        pl.BlockSpec((1, D_MODEL), lambda i, j: (0, 0)),
        pl.BlockSpec((D_MODEL, tf), lambda i, j: (0, j)),
        pl.BlockSpec((D_MODEL, tf), lambda i, j: (0, nf + j)),
        pl.BlockSpec((tf, D_MODEL), lambda i, j: (j, 0)),
    ]
    out_specs = [pl.BlockSpec((tm, D_MODEL), lambda i, j: (i, 0))]
    out_shape = [jax.ShapeDtypeStruct((SEQ, D_MODEL), F32)]
    args = [x, g, w_gu, w_gu, w_down]
    if next_w is not None:
        n_gu, n_down, layer = next_w
        gu_cols = 2 * D_FF // steps
        dn_rows = D_FF // steps
        assert gu_cols % LANES == 0 and dn_rows % 16 == 0
        in_specs += [
            pl.BlockSpec((None, D_MODEL, gu_cols), lambda i, j: (layer, 0, i * nf + j)),
            pl.BlockSpec((None, dn_rows, D_MODEL), lambda i, j: (layer, i * nf + j, 0)),
        ]
        out_specs += [
            pl.BlockSpec((D_MODEL, gu_cols), lambda i, j: (0, i * nf + j)),
            pl.BlockSpec((dn_rows, D_MODEL), lambda i, j: (i * nf + j, 0)),
        ]
        out_shape += [jax.ShapeDtypeStruct((D_MODEL, 2 * D_FF), BF16),
                      jax.ShapeDtypeStruct((D_FF, D_MODEL), BF16)]
        args += [n_gu, n_down]
    return pl.pallas_call(
        functools.partial(_ffn_kernel, convert_next=next_w is not None),
        grid=(SEQ // tm, nf),
        in_specs=in_specs,
        out_specs=out_specs,
        out_shape=out_shape,
        scratch_shapes=[pltpu.VMEM((tm, D_MODEL), BF16)],
        compiler_params=_params(("arbitrary", "arbitrary")),
        name="ffn",
    )(*args)


def kernel(x, mix_norm_g, ffn_norm_g, fox_w_in, fox_b_f, fox_q_norm_g, fox_k_norm_g,
           fox_w_out, pool_w, pool_b, pool_scale, ffn_w_gate_up, ffn_w_down):
    assert x.shape == (1, SEQ, D_MODEL)
    x = x.reshape(SEQ, D_MODEL)
    tri = jnp.tril(jnp.ones((LANES, LANES), BF16))
    q_scale = HEAD_DIM ** -0.5 * LOG2E
    fox_w_in_t = jnp.swapaxes(fox_w_in, 1, 2)
    w_gu = w_down = None
    for i in range(DEPTH):
        j = i // 2
        g_mix = mix_norm_g[i][None, :]
        if i % 2 == 0:
            wf_t = jnp.pad(fox_w_in_t[j, 3 * D_MODEL:, :], ((0, LANES - N_HEADS), (0, 0)))
            bf = jnp.pad(fox_b_f[j], (0, LANES - N_HEADS))[None, :]
            gains = jnp.concatenate([
                jnp.tile(fox_q_norm_g[j] * q_scale, N_HEADS),
                jnp.tile(fox_k_norm_g[j], N_HEADS),
                jnp.ones((D_MODEL,), F32)])[None, :]
            qkv, ct = _fox_proj(x, g_mix, fox_w_in_t, j, gains, wf_t, bf, tri)
            qk_bound = (1.02 * HEAD_DIM * q_scale * jnp.max(jnp.abs(fox_q_norm_g[j]))
                        * jnp.max(jnp.abs(fox_k_norm_g[j])))
            par = jnp.stack([-(SKIP_EXP2 + 2.0 * qk_bound), qk_bound]).astype(F32)
            cend = jnp.pad(ct[:, ATTN_BLOCK - 1::ATTN_BLOCK],
                           ((0, 0), (0, LANES - SEQ // ATTN_BLOCK)))
            if i == 0:
                o, w_gu, w_down = _fox_attn(qkv, ct[:, None, :], cend[:, None, :], par,
                                            (ffn_w_gate_up, ffn_w_down, 0), t=ATTN_BLOCK)
            else:
                o, = _fox_attn(qkv, ct[:, None, :], cend[:, None, :], par, t=ATTN_BLOCK)
            x = _fox_out(x, o, fox_w_out, j)
        else:
            x = _pool_mix(x, g_mix, pool_w, j, pool_b[j][None, :], pool_scale[j][None, :])
        if i + 1 < DEPTH:
            x, w_gu, w_down = _ffn(x, ffn_norm_g[i][None, :], w_gu, w_down,
                                   (ffn_w_gate_up, ffn_w_down, i + 1))
        else:
            x, = _ffn(x, ffn_norm_g[i][None, :], w_gu, w_down)
    return x.reshape(1, SEQ, D_MODEL)
```

```python
import functools
import math

import jax
import jax.numpy as jnp
from jax import lax
from jax.experimental import pallas as pl
from jax.experimental.pallas import tpu as pltpu

D_MODEL = 2048
SEQ = 8192
DEPTH = 4
HEAD_DIM = 128
N_HEADS = D_MODEL // HEAD_DIM
POOL_WINDOWS = (2, 4, 8, 16)
POOL_GROUP = D_MODEL // len(POOL_WINDOWS)
POOL_HALO = 32
D_FF = 5632
RMS_EPS = 1e-6
NEG_INF = -1e30
LOG2E = math.log2(math.e)

LANES = 128
VMEM_LIMIT = 62 * 1024 * 1024

F32 = jnp.float32
BF16 = jnp.bfloat16


def _params(semantics):
    return pltpu.CompilerParams(dimension_semantics=semantics,
                                vmem_limit_bytes=VMEM_LIMIT)


def _rmsnorm(x, g):
    return x * lax.rsqrt(jnp.mean(x * x, axis=-1, keepdims=True) + RMS_EPS) * g


def _split3_bf16(x):
    x1 = x.astype(BF16)
    r1 = x - x1.astype(F32)
    x2 = r1.astype(BF16)
    x3 = (r1 - x2.astype(F32)).astype(BF16)
    return x1, x2, x3


ROW_CHUNK = 256
_NT_DIMS = (((1,), (1,)), ((), ()))


def _fox_proj_kernel(x_ref, g_ref, w_ref, gain_ref, wf_ref, bf_ref, tri_ref,
                     qkv_ref, ct_ref, h_scr, carry_scr, *, tm, tn):
    i = pl.program_id(0)
    j = pl.program_id(1)

    @pl.when(j == 0)
    def _():
        hb = _rmsnorm(x_ref[...], g_ref[...]).astype(BF16)
        h_scr[...] = hb
        f = lax.dot_general(hb, wf_ref[...].astype(BF16), _NT_DIMS,
                            preferred_element_type=F32) + bf_ref[...]
        logf = jnp.minimum(f, 0.0) - jnp.log1p(jnp.exp(-jnp.abs(f)))

        @pl.when(i == 0)
        def _():
            carry_scr[...] = jnp.zeros_like(carry_scr)

        tri = tri_ref[...]
        carry = carry_scr[0:1, :]
        chunks = []
        for r in range(tm // LANES):
            x1, x2, x3 = _split3_bf16(logf[r * LANES:(r + 1) * LANES, :])
            cs = (jnp.dot(tri, x1, preferred_element_type=F32)
                  + jnp.dot(tri, x2, preferred_element_type=F32)
                  + jnp.dot(tri, x3, preferred_element_type=F32)) + carry
            carry = cs[LANES - 1:LANES, :]
            chunks.append(cs)
        carry_scr[0:1, :] = carry
        c = jnp.concatenate(chunks, axis=0) * LOG2E
        ct_ref[...] = c.T[:N_HEADS, :]

    w_t = w_ref[...].astype(BF16)
    is_qk = (j < 2 * D_MODEL // tn).astype(F32)
    for rc in range(tm // ROW_CHUNK):
        rows = slice(rc * ROW_CHUNK, (rc + 1) * ROW_CHUNK)
        r = lax.dot_general(h_scr[rows, :], w_t, _NT_DIMS,
                            preferred_element_type=F32)
        for hh in range(tn // HEAD_DIM):
            sl = slice(hh * HEAD_DIM, (hh + 1) * HEAD_DIM)
            rh = r[:, sl]
            inv = lax.rsqrt(jnp.mean(rh * rh, axis=-1, keepdims=True) + RMS_EPS)
            scale = is_qk * inv + (1.0 - is_qk)
            qkv_ref[rows, sl] = (rh * scale * gain_ref[:, sl]).astype(BF16)


def _fox_proj(x, g, w_in_t, layer, gains, wf_t, bf, tri, *, tm=1024, tn=1024):
    n = 3 * D_MODEL
    return pl.pallas_call(
        functools.partial(_fox_proj_kernel, tm=tm, tn=tn),
        grid=(SEQ // tm, n // tn),
        in_specs=[
            pl.BlockSpec((tm, D_MODEL), lambda i, j: (i, 0)),
            pl.BlockSpec((1, D_MODEL), lambda i, j: (0, 0)),
            pl.BlockSpec((None, tn, D_MODEL), lambda i, j: (layer, j, 0)),
            pl.BlockSpec((1, tn), lambda i, j: (0, j)),
            pl.BlockSpec((LANES, D_MODEL), lambda i, j: (0, 0)),
            pl.BlockSpec((1, LANES), lambda i, j: (0, 0)),
            pl.BlockSpec((LANES, LANES), lambda i, j: (0, 0)),
        ],
        out_specs=[
            pl.BlockSpec((tm, tn), lambda i, j: (i, j)),
            pl.BlockSpec((N_HEADS, tm), lambda i, j: (0, i)),
        ],
        out_shape=[
            jax.ShapeDtypeStruct((SEQ, n), BF16),
            jax.ShapeDtypeStruct((N_HEADS, SEQ), F32),
        ],
        scratch_shapes=[
            pltpu.VMEM((tm, D_MODEL), BF16),
            pltpu.VMEM((8, LANES), F32),
        ],
        compiler_params=_params(("arbitrary", "arbitrary")),
        name="fox_proj",
    )(x, g, w_in_t, gains, wf_t, bf, tri)


SKIP_EXP2 = 151.0
ATTN_BLOCK = 512
MAX_FIXED_SHIFT_BOUND = 48.0


def _fox_attn_kernel(*refs, t, convert):
    if convert:
        (par_ref, q_ref, k_ref, v_ref, c_ref, cend_ref, wa_ref, wb_ref,
         o_ref, wa_out, wb_out, m_scr, l_scr, acc_scr) = refs
        wa_out[...] = wa_ref[...].astype(BF16)
        wb_out[...] = wb_ref[...].astype(BF16)
    else:
        (par_ref, q_ref, k_ref, v_ref, c_ref, cend_ref,
         o_ref, m_scr, l_scr, acc_scr) = refs
    blk = lax.broadcasted_iota(jnp.int32, (1, LANES), 1)
    cend = cend_ref[0]
    thr = par_ref[0]
    qk_bound = par_ref[1]

    def causal(x):
        row = lax.broadcasted_iota(jnp.int32, (t, t), 0)
        col = lax.broadcasted_iota(jnp.int32, (t, t), 1)
        return jnp.where(row >= col, x, NEG_INF)

    def scores(q, c_q0, j):
        k0 = pl.multiple_of(j * t, t)
        s = lax.dot_general(q, k_ref[pl.ds(k0, t), :], _NT_DIMS,
                            preferred_element_type=F32)
        return s + (c_q0 - c_ref[0, :, pl.ds(k0, t)])

    def v_block(j):
        return v_ref[pl.ds(pl.multiple_of(j * t, t), t), :]

    def q_block(qi, fixed_shift):
        q0 = pl.multiple_of(qi * t, t)
        q = q_ref[pl.ds(q0, t), :]
        c_q = c_ref[0, :, pl.ds(q0, t)]
        c_q0 = c_q[:, 0:1]
        skippable = ((c_q0 - cend) < thr) & (blk < qi)
        first = jnp.sum(skippable.astype(jnp.int32))
        l_scr[...] = jnp.zeros_like(l_scr)
        acc_scr[...] = jnp.zeros_like(acc_scr)

        if fixed_shift:
            shift = qk_bound + (c_q0 - c_q)
            shift = jnp.broadcast_to(shift, (LANES, t)).T
            shift = jnp.concatenate([shift] * (t // LANES), axis=1)

            def step(j, masked):
                x = scores(q, c_q0, j) - shift
                p = jnp.exp2(causal(x) if masked else x)
                part = p[:, 0:LANES]
                for n in range(1, t // LANES):
                    part = part + p[:, n * LANES:(n + 1) * LANES]
                l_scr[...] += part
                acc_scr[...] += jnp.dot(p.astype(BF16), v_block(j),
                                        preferred_element_type=F32)
        else:
            m_scr[...] = jnp.full_like(m_scr, NEG_INF)

            def step(j, masked):
                s = scores(q, c_q0, j)
                if masked:
                    s = causal(s)
                m_prev = m_scr[...]
                m_next = jnp.maximum(m_prev, jnp.max(s, axis=1, keepdims=True))
                p = jnp.exp2(s - jnp.concatenate([m_next] * (t // LANES), axis=1))
                alpha = jnp.exp2(m_prev - m_next)
                l_scr[...] = alpha * l_scr[...] + jnp.sum(p, axis=1, keepdims=True)
                m_scr[...] = m_next
                acc_scr[...] = alpha * acc_scr[...] + jnp.dot(
                    p.astype(BF16), v_block(j), preferred_element_type=F32)

        step(qi, True)

        n_blocks = qi - first
        n_quads = n_blocks // 4

        def kv_quad(n, c):
            j = first + 4 * n
            for d in range(4):
                step(j + d, False)
            return c

        lax.fori_loop(0, n_quads, kv_quad, 0)
        rest = first + 4 * n_quads

        @pl.when(n_blocks - 4 * n_quads >= 2)
        def _():
            step(rest, False)
            step(rest + 1, False)

        @pl.when(n_blocks % 2 == 1)
        def _():
            step(qi - 1, False)

        l = l_scr[...]
        if fixed_shift:
            l = jnp.sum(l, axis=1, keepdims=True)
        o_ref[pl.ds(q0, t), :] = (acc_scr[...] / l).astype(BF16)

    def all_q_blocks(fixed_shift):
        def body(qi, carry):
            q_block(qi, fixed_shift)
            return carry
        lax.fori_loop(0, SEQ // t, body, 0)

    @pl.when(qk_bound < MAX_FIXED_SHIFT_BOUND)
    def _():
        all_q_blocks(True)

    @pl.when(qk_bound >= MAX_FIXED_SHIFT_BOUND)
    def _():
        all_q_blocks(False)


def _fox_attn(qkv, c3, cend3, par, cast_w=None, *, t=512):
    in_specs = [
        pl.BlockSpec(memory_space=pltpu.SMEM),
        pl.BlockSpec((SEQ, HEAD_DIM), lambda h: (0, h)),
        pl.BlockSpec((SEQ, HEAD_DIM), lambda h: (0, N_HEADS + h)),
        pl.BlockSpec((SEQ, HEAD_DIM), lambda h: (0, 2 * N_HEADS + h)),
        pl.BlockSpec((1, 1, SEQ), lambda h: (h, 0, 0)),
        pl.BlockSpec((1, 1, LANES), lambda h: (h, 0, 0)),
    ]
    out_specs = [pl.BlockSpec((SEQ, HEAD_DIM), lambda h: (0, h))]
    out_shape = [jax.ShapeDtypeStruct((SEQ, D_MODEL), BF16)]
    args = [par, qkv, qkv, qkv, c3, cend3]
    if cast_w is not None:
        wa, wb, layer = cast_w
        for w in (wa, wb):
            rows, cols = w.shape[1] // N_HEADS, w.shape[2]
            assert rows % 16 == 0
            in_specs.append(pl.BlockSpec((None, rows, cols), lambda h: (layer, h, 0)))
            out_specs.append(pl.BlockSpec((rows, cols), lambda h: (h, 0)))
            out_shape.append(jax.ShapeDtypeStruct(w.shape[1:], BF16))
        args += [wa, wb]
    return pl.pallas_call(
        functools.partial(_fox_attn_kernel, t=t, convert=cast_w is not None),
        grid=(N_HEADS,),
        in_specs=in_specs,
        out_specs=out_specs,
        out_shape=out_shape,
        scratch_shapes=[
            pltpu.VMEM((t, LANES), F32),
            pltpu.VMEM((t, LANES), F32),
            pltpu.VMEM((t, HEAD_DIM), F32),
        ],
        compiler_params=_params(("arbitrary",)),
        name="fox_attn",
    )(*args)


def _fox_out_kernel(x_ref, o_ref, w_ref, y_ref, wb_scr):
    @pl.when(pl.program_id(0) == 0)
    def _():
        wb_scr[...] = w_ref[...].astype(BF16)

    y_ref[...] = x_ref[...] + jnp.dot(o_ref[...], wb_scr[...],
                                      preferred_element_type=F32)


def _fox_out(x, o, w_out, layer, *, tm=512):
    return pl.pallas_call(
        _fox_out_kernel,
        grid=(SEQ // tm,),
        in_specs=[
            pl.BlockSpec((tm, D_MODEL), lambda i: (i, 0)),
            pl.BlockSpec((tm, D_MODEL), lambda i: (i, 0)),
            pl.BlockSpec((None, D_MODEL, D_MODEL), lambda i: (layer, 0, 0),
                         pipeline_mode=pl.Buffered(1)),
        ],
        out_specs=pl.BlockSpec((tm, D_MODEL), lambda i: (i, 0)),
        out_shape=jax.ShapeDtypeStruct((SEQ, D_MODEL), F32),
        scratch_shapes=[pltpu.VMEM((D_MODEL, D_MODEL), BF16)],
        compiler_params=_params(("arbitrary",)),
        name="fox_out",
    )(x, o, w_out)


def _pool_kernel(x_ref, g_ref, w_ref, b_ref, s_ref, y_ref, h_scr, a_scr, b_scr, wb_scr,
                 *, tm):
    i = pl.program_id(0)
    end = POOL_HALO + tm

    @pl.when(i == 0)
    def _():
        h_scr[0:POOL_HALO, :] = jnp.zeros((POOL_HALO, D_MODEL), F32)
        wb_scr[...] = w_ref[...].astype(BF16)

    x = x_ref[...]
    h_scr[POOL_HALO:end, :] = _rmsnorm(x, g_ref[...])
    t = i * tm + lax.broadcasted_iota(jnp.int32, (tm, 1), 0)
    for gi, win in enumerate(POOL_WINDOWS):
        cols = slice(gi * POOL_GROUP, (gi + 1) * POOL_GROUP)
        h = h_scr[POOL_HALO:end, cols]
        levels = win.bit_length() - 1
        src = lambda lo, hi: h_scr[lo:hi, cols]
        for k in range(1, levels + 1):
            lo = POOL_HALO - 8 * (levels - k)
            shift = 2 ** (k - 1)
            tot = src(lo, end) + src(lo - shift, end - shift)
            if k < levels:
                buf = a_scr if k % 2 else b_scr
                buf[lo:end, :] = tot
                src = lambda lo, hi, buf=buf: buf[lo:hi, :]
        cnt = jnp.minimum(t + 1, win).astype(F32)
        y = (tot / cnt - h).astype(BF16)
        y = jnp.dot(y, wb_scr[gi], preferred_element_type=F32)
        y_ref[:, cols] = x[:, cols] + (y + b_ref[:, cols]) * s_ref[:, cols]
    h_scr[0:POOL_HALO, :] = h_scr[tm:tm + POOL_HALO, :]


def _pool_mix(x, g, w, layer, b, scale, *, tm=512):
    n_groups = len(POOL_WINDOWS)
    return pl.pallas_call(
        functools.partial(_pool_kernel, tm=tm),
        grid=(SEQ // tm,),
        in_specs=[
            pl.BlockSpec((tm, D_MODEL), lambda i: (i, 0)),
            pl.BlockSpec((1, D_MODEL), lambda i: (0, 0)),
            pl.BlockSpec((None, n_groups, POOL_GROUP, POOL_GROUP),
                         lambda i: (layer, 0, 0, 0), pipeline_mode=pl.Buffered(1)),
            pl.BlockSpec((1, D_MODEL), lambda i: (0, 0)),
            pl.BlockSpec((1, D_MODEL), lambda i: (0, 0)),
        ],
        out_specs=pl.BlockSpec((tm, D_MODEL), lambda i: (i, 0)),
        out_shape=jax.ShapeDtypeStruct((SEQ, D_MODEL), F32),
        scratch_shapes=[pltpu.VMEM((POOL_HALO + tm, D_MODEL), F32),
                        pltpu.VMEM((POOL_HALO + tm, POOL_GROUP), F32),
                        pltpu.VMEM((POOL_HALO + tm, POOL_GROUP), F32),
                        pltpu.VMEM((n_groups, POOL_GROUP, POOL_GROUP), BF16)],
        compiler_params=_params(("arbitrary",)),
        name="pool_mix",
    )(x, g, w, b, scale)


FFN_ROW_CHUNK = 256


def _ffn_kernel(*refs, convert_next):
    if convert_next:
        (x_ref, g_ref, wg_ref, wu_ref, wd_ref, ngu_ref, ndn_ref,
         y_ref, ngu_out, ndn_out, h_scr) = refs
        ngu_out[...] = ngu_ref[...].astype(BF16)
        ndn_out[...] = ndn_ref[...].astype(BF16)
    else:
        x_ref, g_ref, wg_ref, wu_ref, wd_ref, y_ref, h_scr = refs
    j = pl.program_id(1)

    @pl.when(j == 0)
    def _():
        x = x_ref[...]
        h_scr[...] = _rmsnorm(x, g_ref[...]).astype(BF16)
        y_ref[...] = x

    chunks = [slice(r, r + FFN_ROW_CHUNK) for r in range(0, x_ref.shape[0], FFN_ROW_CHUNK)]
    acts = []
    for rows in chunks:
        h = h_scr[rows, :]
        gate = jnp.dot(h, wg_ref[...], preferred_element_type=F32)
        up = jnp.dot(h, wu_ref[...], preferred_element_type=F32)
        acts.append((gate * jax.nn.sigmoid(gate) * up).astype(BF16))
    for rows, a in zip(chunks, acts):
        y_ref[rows, :] += jnp.dot(a, wd_ref[...], preferred_element_type=F32)


def _ffn(x, g, w_gu, w_down, next_w=None, *, tm=1024, tf=512):
    nf = D_FF // tf
    steps = (SEQ // tm) * nf
    in_specs = [
        pl.BlockSpec((tm, D_MODEL), lambda i, j: (i, 0)),
        pl.BlockSpec((1, D_MODEL), lambda i, j: (0, 0)),
        pl.BlockSpec((D_MODEL, tf), lambda i, j: (0, j)),
        pl.BlockSpec((D_MODEL, tf), lambda i, j: (0, nf + j)),
        pl.BlockSpec((tf, D_MODEL), lambda i, j: (j, 0)),
    ]
    out_specs = [pl.BlockSpec((tm, D_MODEL), lambda i, j: (i, 0))]
    out_shape = [jax.ShapeDtypeStruct((SEQ, D_MODEL), F32)]
    args = [x, g, w_gu, w_gu, w_down]
    if next_w is not None:
        n_gu, n_down, layer = next_w
        gu_cols = 2 * D_FF // steps
        dn_rows = D_FF // steps
        assert gu_cols % LANES == 0 and dn_rows % 16 == 0
        in_specs += [
            pl.BlockSpec((None, D_MODEL, gu_cols), lambda i, j: (layer, 0, i * nf + j)),
            pl.BlockSpec((None, dn_rows, D_MODEL), lambda i, j: (layer, i * nf + j, 0)),
        ]
        out_specs += [
            pl.BlockSpec((D_MODEL, gu_cols), lambda i, j: (0, i * nf + j)),
            pl.BlockSpec((dn_rows, D_MODEL), lambda i, j: (i * nf + j, 0)),
        ]
        out_shape += [jax.ShapeDtypeStruct((D_MODEL, 2 * D_FF), BF16),
                      jax.ShapeDtypeStruct((D_FF, D_MODEL), BF16)]
        args += [n_gu, n_down]
    return pl.pallas_call(
        functools.partial(_ffn_kernel, convert_next=next_w is not None),
        grid=(SEQ // tm, nf),
        in_specs=in_specs,
        out_specs=out_specs,
        out_shape=out_shape,
        scratch_shapes=[pltpu.VMEM((tm, D_MODEL), BF16)],
        compiler_params=_params(("arbitrary", "arbitrary")),
        name="ffn",
    )(*args)


def kernel(x, mix_norm_g, ffn_norm_g, fox_w_in, fox_b_f, fox_q_norm_g, fox_k_norm_g,
           fox_w_out, pool_w, pool_b, pool_scale, ffn_w_gate_up, ffn_w_down):
    assert x.shape == (1, SEQ, D_MODEL)
    x = x.reshape(SEQ, D_MODEL)
    tri = jnp.tril(jnp.ones((LANES, LANES), BF16))
    q_scale = HEAD_DIM ** -0.5 * LOG2E
    fox_w_in_t = jnp.swapaxes(fox_w_in, 1, 2)
    w_gu = w_down = None
    for i in range(DEPTH):
        j = i // 2
        g_mix = mix_norm_g[i][None, :]
        if i % 2 == 0:
            wf_t = jnp.pad(fox_w_in_t[j, 3 * D_MODEL:, :], ((0, LANES - N_HEADS), (0, 0)))
            bf = jnp.pad(fox_b_f[j], (0, LANES - N_HEADS))[None, :]
            gains = jnp.concatenate([
                jnp.tile(fox_q_norm_g[j] * q_scale, N_HEADS),
                jnp.tile(fox_k_norm_g[j], N_HEADS),
                jnp.ones((D_MODEL,), F32)])[None, :]
            qkv, ct = _fox_proj(x, g_mix, fox_w_in_t, j, gains, wf_t, bf, tri)
            qk_bound = (1.02 * HEAD_DIM * q_scale * jnp.max(jnp.abs(fox_q_norm_g[j]))
                        * jnp.max(jnp.abs(fox_k_norm_g[j])))
            par = jnp.stack([-(SKIP_EXP2 + 2.0 * qk_bound), qk_bound]).astype(F32)
            cend = jnp.pad(ct[:, ATTN_BLOCK - 1::ATTN_BLOCK],
                           ((0, 0), (0, LANES - SEQ // ATTN_BLOCK)))
            if i == 0:
                o, w_gu, w_down = _fox_attn(qkv, ct[:, None, :], cend[:, None, :], par,
                                            (ffn_w_gate_up, ffn_w_down, 0), t=ATTN_BLOCK)
            else:
                o, = _fox_attn(qkv, ct[:, None, :], cend[:, None, :], par, t=ATTN_BLOCK)
            x = _fox_out(x, o, fox_w_out, j)
        else:
            x = _pool_mix(x, g_mix, pool_w, j, pool_b[j][None, :], pool_scale[j][None, :])
        if i + 1 < DEPTH:
            x, w_gu, w_down = _ffn(x, ffn_norm_g[i][None, :], w_gu, w_down,
                                   (ffn_w_gate_up, ffn_w_down, i + 1))
        else:
            x, = _ffn(x, ffn_norm_g[i][None, :], w_gu, w_down)
    return x.reshape(1, SEQ, D_MODEL)
```

```python
import functools
import math

import jax
import jax.numpy as jnp
from jax import lax
from jax.experimental import pallas as pl
from jax.experimental.pallas import tpu as pltpu

D_MODEL = 2048
SEQ = 8192
DEPTH = 4
HEAD_DIM = 128
N_HEADS = D_MODEL // HEAD_DIM
POOL_WINDOWS = (2, 4, 8, 16)
POOL_GROUP = D_MODEL // len(POOL_WINDOWS)
POOL_HALO = 32
D_FF = 5632
RMS_EPS = 1e-6
NEG_INF = -1e30
LOG2E = math.log2(math.e)

LANES = 128
VMEM_LIMIT = 62 * 1024 * 1024

F32 = jnp.float32
BF16 = jnp.bfloat16


def _params(semantics):
    return pltpu.CompilerParams(dimension_semantics=semantics,
                                vmem_limit_bytes=VMEM_LIMIT)


def _rmsnorm(x, g):
    return x * lax.rsqrt(jnp.mean(x * x, axis=-1, keepdims=True) + RMS_EPS) * g


def _split3_bf16(x):
    x1 = x.astype(BF16)
    r1 = x - x1.astype(F32)
    x2 = r1.astype(BF16)
    x3 = (r1 - x2.astype(F32)).astype(BF16)
    return x1, x2, x3


ROW_CHUNK = 256
_NT_DIMS = (((1,), (1,)), ((), ()))


def _fox_proj_kernel(x_ref, g_ref, w_ref, gain_ref, wf_ref, bf_ref, tri_ref,
                     qkv_ref, ct_ref, h_scr, carry_scr, *, tm, tn):
    i = pl.program_id(0)
    j = pl.program_id(1)

    @pl.when(j == 0)
    def _():
        hb = _rmsnorm(x_ref[...], g_ref[...]).astype(BF16)
        h_scr[...] = hb
        f = lax.dot_general(hb, wf_ref[...].astype(BF16), _NT_DIMS,
                            preferred_element_type=F32) + bf_ref[...]
        logf = jnp.minimum(f, 0.0) - jnp.log1p(jnp.exp(-jnp.abs(f)))

        @pl.when(i == 0)
        def _():
            carry_scr[...] = jnp.zeros_like(carry_scr)

        tri = tri_ref[...]
        carry = carry_scr[0:1, :]
        chunks = []
        for r in range(tm // LANES):
            x1, x2, x3 = _split3_bf16(logf[r * LANES:(r + 1) * LANES, :])
            cs = (jnp.dot(tri, x1, preferred_element_type=F32)
                  + jnp.dot(tri, x2, preferred_element_type=F32)
                  + jnp.dot(tri, x3, preferred_element_type=F32)) + carry
            carry = cs[LANES - 1:LANES, :]
            chunks.append(cs)
        carry_scr[0:1, :] = carry
        c = jnp.concatenate(chunks, axis=0) * LOG2E
        ct_ref[...] = c.T[:N_HEADS, :]

    w_t = w_ref[...].astype(BF16)
    is_qk = (j < 2 * D_MODEL // tn).astype(F32)
    for rc in range(tm // ROW_CHUNK):
        rows = slice(rc * ROW_CHUNK, (rc + 1) * ROW_CHUNK)
        r = lax.dot_general(h_scr[rows, :], w_t, _NT_DIMS,
                            preferred_element_type=F32)
        for hh in range(tn // HEAD_DIM):
            sl = slice(hh * HEAD_DIM, (hh + 1) * HEAD_DIM)
            rh = r[:, sl]
            inv = lax.rsqrt(jnp.mean(rh * rh, axis=-1, keepdims=True) + RMS_EPS)
            scale = is_qk * inv + (1.0 - is_qk)
            qkv_ref[rows, sl] = (rh * scale * gain_ref[:, sl]).astype(BF16)


def _fox_proj(x, g, w_in_t, layer, gains, wf_t, bf, tri, *, tm=1024, tn=1024):
    n = 3 * D_MODEL
    return pl.pallas_call(
        functools.partial(_fox_proj_kernel, tm=tm, tn=tn),
        grid=(SEQ // tm, n // tn),
        in_specs=[
            pl.BlockSpec((tm, D_MODEL), lambda i, j: (i, 0)),
            pl.BlockSpec((1, D_MODEL), lambda i, j: (0, 0)),
            pl.BlockSpec((None, tn, D_MODEL), lambda i, j: (layer, j, 0)),
            pl.BlockSpec((1, tn), lambda i, j: (0, j)),
            pl.BlockSpec((LANES, D_MODEL), lambda i, j: (0, 0)),
            pl.BlockSpec((1, LANES), lambda i, j: (0, 0)),
            pl.BlockSpec((LANES, LANES), lambda i, j: (0, 0)),
        ],
        out_specs=[
            pl.BlockSpec((tm, tn), lambda i, j: (i, j)),
            pl.BlockSpec((N_HEADS, tm), lambda i, j: (0, i)),
        ],
        out_shape=[
            jax.ShapeDtypeStruct((SEQ, n), BF16),
            jax.ShapeDtypeStruct((N_HEADS, SEQ), F32),
        ],
        scratch_shapes=[
            pltpu.VMEM((tm, D_MODEL), BF16),
            pltpu.VMEM((8, LANES), F32),
        ],
        compiler_params=_params(("arbitrary", "arbitrary")),
        name="fox_proj",
    )(x, g, w_in_t, gains, wf_t, bf, tri)


SKIP_EXP2 = 151.0
ATTN_BLOCK = 512
MAX_FIXED_SHIFT_BOUND = 48.0


def _fox_attn_kernel(*refs, t, convert):
    if convert:
        (par_ref, q_ref, k_ref, v_ref, c_ref, cend_ref, wa_ref, wb_ref,
         o_ref, wa_out, wb_out, m_scr, l_scr, acc_scr) = refs
        wa_out[...] = wa_ref[...].astype(BF16)
        wb_out[...] = wb_ref[...].astype(BF16)
    else:
        (par_ref, q_ref, k_ref, v_ref, c_ref, cend_ref,
         o_ref, m_scr, l_scr, acc_scr) = refs
    blk = lax.broadcasted_iota(jnp.int32, (1, LANES), 1)
    cend = cend_ref[0]
    thr = par_ref[0]
    qk_bound = par_ref[1]

    def causal(x):
        row = lax.broadcasted_iota(jnp.int32, (t, t), 0)
        col = lax.broadcasted_iota(jnp.int32, (t, t), 1)
        return jnp.where(row >= col, x, NEG_INF)

    def scores(q, c_q0, j):
        k0 = pl.multiple_of(j * t, t)
        s = lax.dot_general(q, k_ref[pl.ds(k0, t), :], _NT_DIMS,
                            preferred_element_type=F32)
        return s + (c_q0 - c_ref[0, :, pl.ds(k0, t)])

    def v_block(j):
        return v_ref[pl.ds(pl.multiple_of(j * t, t), t), :]

    def q_block(qi, fixed_shift):
        q0 = pl.multiple_of(qi * t, t)
        q = q_ref[pl.ds(q0, t), :]
        c_q = c_ref[0, :, pl.ds(q0, t)]
        c_q0 = c_q[:, 0:1]
        skippable = ((c_q0 - cend) < thr) & (blk < qi)
        first = jnp.sum(skippable.astype(jnp.int32))
        l_scr[...] = jnp.zeros_like(l_scr)
        acc_scr[...] = jnp.zeros_like(acc_scr)

        if fixed_shift:
            shift = qk_bound + (c_q0 - c_q)
            shift = jnp.broadcast_to(shift, (LANES, t)).T
            shift = jnp.concatenate([shift] * (t // LANES), axis=1)

            def step(j, masked):
                x = scores(q, c_q0, j) - shift
                p = jnp.exp2(causal(x) if masked else x)
                part = p[:, 0:LANES]
                for n in range(1, t // LANES):
                    part = part + p[:, n * LANES:(n + 1) * LANES]
                l_scr[...] += part
                acc_scr[...] += jnp.dot(p.astype(BF16), v_block(j),
                                        preferred_element_type=F32)
        else:
            m_scr[...] = jnp.full_like(m_scr, NEG_INF)

            def step(j, masked):
                s = scores(q, c_q0, j)
                if masked:
                    s = causal(s)
                m_prev = m_scr[...]
                m_next = jnp.maximum(m_prev, jnp.max(s, axis=1, keepdims=True))
                p = jnp.exp2(s - jnp.concatenate([m_next] * (t // LANES), axis=1))
                alpha = jnp.exp2(m_prev - m_next)
                l_scr[...] = alpha * l_scr[...] + jnp.sum(p, axis=1, keepdims=True)
                m_scr[...] = m_next
                acc_scr[...] = alpha * acc_scr[...] + jnp.dot(
                    p.astype(BF16), v_block(j), preferred_element_type=F32)

        step(qi, True)

        n_pairs = (qi - first) // 2

        def kv_pair(n, c):
            j = first + 2 * n
            step(j, False)
            step(j + 1, False)
            return c

        lax.fori_loop(0, n_pairs, kv_pair, 0)

        def finalize():
            l = l_scr[...]
            if fixed_shift:
                l = jnp.sum(l, axis=1, keepdims=True)
            o_ref[pl.ds(q0, t), :] = (acc_scr[...] / l).astype(BF16)

        odd = first + 2 * n_pairs < qi

        @pl.when(odd)
        def _():
            step(qi - 1, False)
            finalize()

        @pl.when(jnp.logical_not(odd))
        def _():
            finalize()

    def all_q_blocks(fixed_shift):
        def body(qi, carry):
            q_block(qi, fixed_shift)
            return carry
        lax.fori_loop(0, SEQ // t, body, 0)

    @pl.when(qk_bound < MAX_FIXED_SHIFT_BOUND)
    def _():
        all_q_blocks(True)

    @pl.when(qk_bound >= MAX_FIXED_SHIFT_BOUND)
    def _():
        all_q_blocks(False)


def _fox_attn(qkv, c3, cend3, par, cast_w=None, *, t=512):
    in_specs = [
        pl.BlockSpec(memory_space=pltpu.SMEM),
        pl.BlockSpec((SEQ, HEAD_DIM), lambda h: (0, h)),
        pl.BlockSpec((SEQ, HEAD_DIM), lambda h: (0, N_HEADS + h)),
        pl.BlockSpec((SEQ, HEAD_DIM), lambda h: (0, 2 * N_HEADS + h)),
        pl.BlockSpec((1, 1, SEQ), lambda h: (h, 0, 0)),
        pl.BlockSpec((1, 1, LANES), lambda h: (h, 0, 0)),
    ]
    out_specs = [pl.BlockSpec((SEQ, HEAD_DIM), lambda h: (0, h))]
    out_shape = [jax.ShapeDtypeStruct((SEQ, D_MODEL), BF16)]
    args = [par, qkv, qkv, qkv, c3, cend3]
    if cast_w is not None:
        wa, wb, layer = cast_w
        for w in (wa, wb):
            rows, cols = w.shape[1] // N_HEADS, w.shape[2]
            assert rows % 16 == 0
            in_specs.append(pl.BlockSpec((None, rows, cols), lambda h: (layer, h, 0)))
            out_specs.append(pl.BlockSpec((rows, cols), lambda h: (h, 0)))
            out_shape.append(jax.ShapeDtypeStruct(w.shape[1:], BF16))
        args += [wa, wb]
    return pl.pallas_call(
        functools.partial(_fox_attn_kernel, t=t, convert=cast_w is not None),
        grid=(N_HEADS,),
        in_specs=in_specs,
        out_specs=out_specs,
        out_shape=out_shape,
        scratch_shapes=[
            pltpu.VMEM((t, LANES), F32),
            pltpu.VMEM((t, LANES), F32),
            pltpu.VMEM((t, HEAD_DIM), F32),
        ],
        compiler_params=_params(("arbitrary",)),
        name="fox_attn",
    )(*args)


def _fox_out_kernel(x_ref, o_ref, w_ref, y_ref, wb_scr):
    @pl.when(pl.program_id(0) == 0)
    def _():
        wb_scr[...] = w_ref[...].astype(BF16)

    y_ref[...] = x_ref[...] + jnp.dot(o_ref[...], wb_scr[...],
                                      preferred_element_type=F32)


def _fox_out(x, o, w_out, layer, *, tm=512):
    return pl.pallas_call(
        _fox_out_kernel,
        grid=(SEQ // tm,),
        in_specs=[
            pl.BlockSpec((tm, D_MODEL), lambda i: (i, 0)),
            pl.BlockSpec((tm, D_MODEL), lambda i: (i, 0)),
            pl.BlockSpec((None, D_MODEL, D_MODEL), lambda i: (layer, 0, 0),
                         pipeline_mode=pl.Buffered(1)),
        ],
        out_specs=pl.BlockSpec((tm, D_MODEL), lambda i: (i, 0)),
        out_shape=jax.ShapeDtypeStruct((SEQ, D_MODEL), F32),
        scratch_shapes=[pltpu.VMEM((D_MODEL, D_MODEL), BF16)],
        compiler_params=_params(("arbitrary",)),
        name="fox_out",
    )(x, o, w_out)


def _pool_kernel(x_ref, g_ref, w_ref, b_ref, s_ref, y_ref, h_scr, a_scr, b_scr, wb_scr,
                 *, tm):
    i = pl.program_id(0)
    end = POOL_HALO + tm

    @pl.when(i == 0)
    def _():
        h_scr[0:POOL_HALO, :] = jnp.zeros((POOL_HALO, D_MODEL), F32)
        wb_scr[...] = w_ref[...].astype(BF16)

    x = x_ref[...]
    h_scr[POOL_HALO:end, :] = _rmsnorm(x, g_ref[...])
    t = i * tm + lax.broadcasted_iota(jnp.int32, (tm, 1), 0)
    for gi, win in enumerate(POOL_WINDOWS):
        cols = slice(gi * POOL_GROUP, (gi + 1) * POOL_GROUP)
        h = h_scr[POOL_HALO:end, cols]
        levels = win.bit_length() - 1
        src = lambda lo, hi: h_scr[lo:hi, cols]
        for k in range(1, levels + 1):
            lo = POOL_HALO - 8 * (levels - k)
            shift = 2 ** (k - 1)
            tot = src(lo, end) + src(lo - shift, end - shift)
            if k < levels:
                buf = a_scr if k % 2 else b_scr
                buf[lo:end, :] = tot
                src = lambda lo, hi, buf=buf: buf[lo:hi, :]
        cnt = jnp.minimum(t + 1, win).astype(F32)
        y = (tot / cnt - h).astype(BF16)
        y = jnp.dot(y, wb_scr[gi], preferred_element_type=F32)
        y_ref[:, cols] = x[:, cols] + (y + b_ref[:, cols]) * s_ref[:, cols]
    h_scr[0:POOL_HALO, :] = h_scr[tm:tm + POOL_HALO, :]


def _pool_mix(x, g, w, layer, b, scale, *, tm=512):
    n_groups = len(POOL_WINDOWS)
    return pl.pallas_call(
        functools.partial(_pool_kernel, tm=tm),
        grid=(SEQ // tm,),
        in_specs=[
            pl.BlockSpec((tm, D_MODEL), lambda i: (i, 0)),
            pl.BlockSpec((1, D_MODEL), lambda i: (0, 0)),
            pl.BlockSpec((None, n_groups, POOL_GROUP, POOL_GROUP),
                         lambda i: (layer, 0, 0, 0), pipeline_mode=pl.Buffered(1)),
            pl.BlockSpec((1, D_MODEL), lambda i: (0, 0)),
            pl.BlockSpec((1, D_MODEL), lambda i: (0, 0)),
        ],
        out_specs=pl.BlockSpec((tm, D_MODEL), lambda i: (i, 0)),
        out_shape=jax.ShapeDtypeStruct((SEQ, D_MODEL), F32),
        scratch_shapes=[pltpu.VMEM((POOL_HALO + tm, D_MODEL), F32),
                        pltpu.VMEM((POOL_HALO + tm, POOL_GROUP), F32),
                        pltpu.VMEM((POOL_HALO + tm, POOL_GROUP), F32),
                        pltpu.VMEM((n_groups, POOL_GROUP, POOL_GROUP), BF16)],
        compiler_params=_params(("arbitrary",)),
        name="pool_mix",
    )(x, g, w, b, scale)


FFN_ROW_CHUNK = 256


def _ffn_kernel(*refs, convert_next):
    if convert_next:
        (x_ref, g_ref, wg_ref, wu_ref, wd_ref, ngu_ref, ndn_ref,
         y_ref, ngu_out, ndn_out, h_scr) = refs
        ngu_out[...] = ngu_ref[...].astype(BF16)
        ndn_out[...] = ndn_ref[...].astype(BF16)
    else:
        x_ref, g_ref, wg_ref, wu_ref, wd_ref, y_ref, h_scr = refs
    j = pl.program_id(1)

    @pl.when(j == 0)
    def _():
        x = x_ref[...]
        h_scr[...] = _rmsnorm(x, g_ref[...]).astype(BF16)
        y_ref[...] = x

    chunks = [slice(r, r + FFN_ROW_CHUNK) for r in range(0, x_ref.shape[0], FFN_ROW_CHUNK)]
    acts = []
    for rows in chunks:
        h = h_scr[rows, :]
        gate = jnp.dot(h, wg_ref[...], preferred_element_type=F32)
        up = jnp.dot(h, wu_ref[...], preferred_element_type=F32)
        acts.append((gate * jax.nn.sigmoid(gate) * up).astype(BF16))
    for rows, a in zip(chunks, acts):
        y_ref[rows, :] += jnp.dot(a, wd_ref[...], preferred_element_type=F32)


def _ffn(x, g, w_gu, w_down, next_w=None, *, tm=1024, tf=512):
    nf = D_FF // tf
    steps = (SEQ // tm) * nf
    in_specs = [
        pl.BlockSpec((tm, D_MODEL), lambda i, j: (i, 0)),
        pl.BlockSpec((1, D_MODEL), lambda i, j: (0, 0)),
        pl.BlockSpec((D_MODEL, tf), lambda i, j: (0, j)),
        pl.BlockSpec((D_MODEL, tf), lambda i, j: (0, nf + j)),
        pl.BlockSpec((tf, D_MODEL), lambda i, j: (j, 0)),
    ]
    out_specs = [pl.BlockSpec((tm, D_MODEL), lambda i, j: (i, 0))]
    out_shape = [jax.ShapeDtypeStruct((SEQ, D_MODEL), F32)]
    args = [x, g, w_gu, w_gu, w_down]
    if next_w is not None:
        n_gu, n_down, layer = next_w
        gu_cols = 2 * D_FF // steps
        dn_rows = D_FF // steps
        assert gu_cols % LANES == 0 and dn_rows % 16 == 0
        in_specs += [
            pl.BlockSpec((None, D_MODEL, gu_cols), lambda i, j: (layer, 0, i * nf + j)),
            pl.BlockSpec((None, dn_rows, D_MODEL), lambda i, j: (layer, i * nf + j, 0)),
        ]
        out_specs += [
            pl.BlockSpec((D_MODEL, gu_cols), lambda i, j: (0, i * nf + j)),
            pl.BlockSpec((dn_rows, D_MODEL), lambda i, j: (i * nf + j, 0)),
        ]
        out_shape += [jax.ShapeDtypeStruct((D_MODEL, 2 * D_FF), BF16),
                      jax.ShapeDtypeStruct((D_FF, D_MODEL), BF16)]
        args += [n_gu, n_down]
    return pl.pallas_call(
        functools.partial(_ffn_kernel, convert_next=next_w is not None),
        grid=(SEQ // tm, nf),
        in_specs=in_specs,
        out_specs=out_specs,
        out_shape=out_shape,
        scratch_shapes=[pltpu.VMEM((tm, D_MODEL), BF16)],
        compiler_params=_params(("arbitrary", "arbitrary")),
        name="ffn",
    )(*args)


def kernel(x, mix_norm_g, ffn_norm_g, fox_w_in, fox_b_f, fox_q_norm_g, fox_k_norm_g,
           fox_w_out, pool_w, pool_b, pool_scale, ffn_w_gate_up, ffn_w_down):
    assert x.shape == (1, SEQ, D_MODEL)
    x = x.reshape(SEQ, D_MODEL)
    tri = jnp.tril(jnp.ones((LANES, LANES), BF16))
    q_scale = HEAD_DIM ** -0.5 * LOG2E
    fox_w_in_t = jnp.swapaxes(fox_w_in, 1, 2)
    w_gu = w_down = None
    for i in range(DEPTH):
        j = i // 2
        g_mix = mix_norm_g[i][None, :]
        if i % 2 == 0:
            wf_t = jnp.pad(fox_w_in_t[j, 3 * D_MODEL:, :], ((0, LANES - N_HEADS), (0, 0)))
            bf = jnp.pad(fox_b_f[j], (0, LANES - N_HEADS))[None, :]
            gains = jnp.concatenate([
                jnp.tile(fox_q_norm_g[j] * q_scale, N_HEADS),
                jnp.tile(fox_k_norm_g[j], N_HEADS),
                jnp.ones((D_MODEL,), F32)])[None, :]
            qkv, ct = _fox_proj(x, g_mix, fox_w_in_t, j, gains, wf_t, bf, tri)
            qk_bound = (1.02 * HEAD_DIM * q_scale * jnp.max(jnp.abs(fox_q_norm_g[j]))
                        * jnp.max(jnp.abs(fox_k_norm_g[j])))
            par = jnp.stack([-(SKIP_EXP2 + 2.0 * qk_bound), qk_bound]).astype(F32)
            cend = jnp.pad(ct[:, ATTN_BLOCK - 1::ATTN_BLOCK],
                           ((0, 0), (0, LANES - SEQ // ATTN_BLOCK)))
            if i == 0:
                o, w_gu, w_down = _fox_attn(qkv, ct[:, None, :], cend[:, None, :], par,
                                            (ffn_w_gate_up, ffn_w_down, 0), t=ATTN_BLOCK)
            else:
                o, = _fox_attn(qkv, ct[:, None, :], cend[:, None, :], par, t=ATTN_BLOCK)
            x = _fox_out(x, o, fox_w_out, j)
        else:
            x = _pool_mix(x, g_mix, pool_w, j, pool_b[j][None, :], pool_scale[j][None, :])
        if i + 1 < DEPTH:
            x, w_gu, w_down = _ffn(x, ffn_norm_g[i][None, :], w_gu, w_down,
                                   (ffn_w_gate_up, ffn_w_down, i + 1))
        else:
            x, = _ffn(x, ffn_norm_g[i][None, :], w_gu, w_down)
    return x.reshape(1, SEQ, D_MODEL)
```

```python
import functools
import math

import jax
import jax.numpy as jnp
from jax import lax
from jax.experimental import pallas as pl
from jax.experimental.pallas import tpu as pltpu

D_MODEL = 2048
SEQ = 8192
DEPTH = 4
HEAD_DIM = 128
N_HEADS = D_MODEL // HEAD_DIM
POOL_WINDOWS = (2, 4, 8, 16)
POOL_GROUP = D_MODEL // len(POOL_WINDOWS)
POOL_HALO = 32
D_FF = 5632
RMS_EPS = 1e-6
NEG_INF = -1e30
LOG2E = math.log2(math.e)

LANES = 128
VMEM_LIMIT = 62 * 1024 * 1024

F32 = jnp.float32
BF16 = jnp.bfloat16


def _params(semantics):
    return pltpu.CompilerParams(dimension_semantics=semantics,
                                vmem_limit_bytes=VMEM_LIMIT)


def _rmsnorm(x, g):
    return x * lax.rsqrt(jnp.mean(x * x, axis=-1, keepdims=True) + RMS_EPS) * g


def _split3_bf16(x):
    x1 = x.astype(BF16)
    r1 = x - x1.astype(F32)
    x2 = r1.astype(BF16)
    x3 = (r1 - x2.astype(F32)).astype(BF16)
    return x1, x2, x3


ROW_CHUNK = 256
_NT_DIMS = (((1,), (1,)), ((), ()))


def _fox_proj_kernel(x_ref, g_ref, w_ref, gain_ref, wf_ref, bf_ref, tri_ref,
                     qkv_ref, ct_ref, h_scr, carry_scr, *, tm, tn):
    i = pl.program_id(0)
    j = pl.program_id(1)

    @pl.when(j == 0)
    def _():
        hb = _rmsnorm(x_ref[...], g_ref[...]).astype(BF16)
        h_scr[...] = hb
        f = lax.dot_general(hb, wf_ref[...].astype(BF16), _NT_DIMS,
                            preferred_element_type=F32) + bf_ref[...]
        logf = jnp.minimum(f, 0.0) - jnp.log1p(jnp.exp(-jnp.abs(f)))

        @pl.when(i == 0)
        def _():
            carry_scr[...] = jnp.zeros_like(carry_scr)

        tri = tri_ref[...]
        carry = carry_scr[0:1, :]
        chunks = []
        for r in range(tm // LANES):
            x1, x2, x3 = _split3_bf16(logf[r * LANES:(r + 1) * LANES, :])
            cs = (jnp.dot(tri, x1, preferred_element_type=F32)
                  + jnp.dot(tri, x2, preferred_element_type=F32)
                  + jnp.dot(tri, x3, preferred_element_type=F32)) + carry
            carry = cs[LANES - 1:LANES, :]
            chunks.append(cs)
        carry_scr[0:1, :] = carry
        c = jnp.concatenate(chunks, axis=0) * LOG2E
        ct_ref[...] = c.T[:N_HEADS, :]

    w_t = w_ref[...].astype(BF16)
    is_qk = (j < 2 * D_MODEL // tn).astype(F32)
    for rc in range(tm // ROW_CHUNK):
        rows = slice(rc * ROW_CHUNK, (rc + 1) * ROW_CHUNK)
        r = lax.dot_general(h_scr[rows, :], w_t, _NT_DIMS,
                            preferred_element_type=F32)
        for hh in range(tn // HEAD_DIM):
            sl = slice(hh * HEAD_DIM, (hh + 1) * HEAD_DIM)
            rh = r[:, sl]
            inv = lax.rsqrt(jnp.mean(rh * rh, axis=-1, keepdims=True) + RMS_EPS)
            scale = is_qk * inv + (1.0 - is_qk)
            qkv_ref[rows, sl] = (rh * scale * gain_ref[:, sl]).astype(BF16)


def _fox_proj(x, g, w_in_t, layer, gains, wf_t, bf, tri, *, tm=1024, tn=1024):
    n = 3 * D_MODEL
    if layer is None:
        w_spec = pl.BlockSpec((tn, D_MODEL), lambda i, j: (j, 0))
    else:
        w_spec = pl.BlockSpec((None, tn, D_MODEL), lambda i, j: (layer, j, 0))
    return pl.pallas_call(
        functools.partial(_fox_proj_kernel, tm=tm, tn=tn),
        grid=(SEQ // tm, n // tn),
        in_specs=[
            pl.BlockSpec((tm, D_MODEL), lambda i, j: (i, 0)),
            pl.BlockSpec((1, D_MODEL), lambda i, j: (0, 0)),
            w_spec,
            pl.BlockSpec((1, tn), lambda i, j: (0, j)),
            pl.BlockSpec((LANES, D_MODEL), lambda i, j: (0, 0)),
            pl.BlockSpec((1, LANES), lambda i, j: (0, 0)),
            pl.BlockSpec((LANES, LANES), lambda i, j: (0, 0)),
        ],
        out_specs=[
            pl.BlockSpec((tm, tn), lambda i, j: (i, j)),
            pl.BlockSpec((N_HEADS, tm), lambda i, j: (0, i)),
        ],
        out_shape=[
            jax.ShapeDtypeStruct((SEQ, n), BF16),
            jax.ShapeDtypeStruct((N_HEADS, SEQ), F32),
        ],
        scratch_shapes=[
            pltpu.VMEM((tm, D_MODEL), BF16),
            pltpu.VMEM((8, LANES), F32),
        ],
        compiler_params=_params(("arbitrary", "arbitrary")),
        name="fox_proj",
    )(x, g, w_in_t, gains, wf_t, bf, tri)


SKIP_EXP2 = 151.0
ATTN_BLOCK = 512
MAX_FIXED_SHIFT_BOUND = 48.0


def _fox_attn_kernel(*refs, t, n_cast):
    par_ref, q_ref, k_ref, v_ref, c_ref, cend_ref = refs[:6]
    w_refs = refs[6:6 + n_cast]
    o_ref = refs[6 + n_cast]
    w_outs = refs[7 + n_cast:7 + 2 * n_cast]
    m_scr, l_scr, acc_scr = refs[7 + 2 * n_cast:]
    for w_ref, w_out in zip(w_refs, w_outs):
        w_out[...] = w_ref[...].astype(BF16)
    blk = lax.broadcasted_iota(jnp.int32, (1, LANES), 1)
    cend = cend_ref[0]
    thr = par_ref[0]
    qk_bound = par_ref[1]

    def causal(x):
        row = lax.broadcasted_iota(jnp.int32, (t, t), 0)
        col = lax.broadcasted_iota(jnp.int32, (t, t), 1)
        return jnp.where(row >= col, x, NEG_INF)

    def scores(q, c_q0, j):
        k0 = pl.multiple_of(j * t, t)
        s = lax.dot_general(q, k_ref[pl.ds(k0, t), :], _NT_DIMS,
                            preferred_element_type=F32)
        return s + (c_q0 - c_ref[0, :, pl.ds(k0, t)])

    def v_block(j):
        return v_ref[pl.ds(pl.multiple_of(j * t, t), t), :]

    def q_block(qi, fixed_shift):
        q0 = pl.multiple_of(qi * t, t)
        q = q_ref[pl.ds(q0, t), :]
        c_q = c_ref[0, :, pl.ds(q0, t)]
        c_q0 = c_q[:, 0:1]
        skippable = ((c_q0 - cend) < thr) & (blk < qi)
        first = jnp.sum(skippable.astype(jnp.int32))
        l_scr[...] = jnp.zeros_like(l_scr)
        acc_scr[...] = jnp.zeros_like(acc_scr)

        if fixed_shift:
            shift = qk_bound + (c_q0 - c_q)
            shift = jnp.broadcast_to(shift, (LANES, t)).T
            shift = jnp.concatenate([shift] * (t // LANES), axis=1)

            def step(j, masked):
                x = scores(q, c_q0, j) - shift
                p = jnp.exp2(causal(x) if masked else x)
                part = p[:, 0:LANES]
                for n in range(1, t // LANES):
                    part = part + p[:, n * LANES:(n + 1) * LANES]
                l_scr[...] += part
                acc_scr[...] += jnp.dot(p.astype(BF16), v_block(j),
                                        preferred_element_type=F32)
        else:
            m_scr[...] = jnp.full_like(m_scr, NEG_INF)

            def step(j, masked):
                s = scores(q, c_q0, j)
                if masked:
                    s = causal(s)
                m_prev = m_scr[...]
                m_next = jnp.maximum(m_prev, jnp.max(s, axis=1, keepdims=True))
                p = jnp.exp2(s - jnp.concatenate([m_next] * (t // LANES), axis=1))
                alpha = jnp.exp2(m_prev - m_next)
                l_scr[...] = alpha * l_scr[...] + jnp.sum(p, axis=1, keepdims=True)
                m_scr[...] = m_next
                acc_scr[...] = alpha * acc_scr[...] + jnp.dot(
                    p.astype(BF16), v_block(j), preferred_element_type=F32)

        step(qi, True)

        n_pairs = (qi - first) // 2

        def kv_pair(n, c):
            j = first + 2 * n
            step(j, False)
            step(j + 1, False)
            return c

        lax.fori_loop(0, n_pairs, kv_pair, 0)

        def finalize():
            l = l_scr[...]
            if fixed_shift:
                l = jnp.sum(l, axis=1, keepdims=True)
            o_ref[pl.ds(q0, t), :] = (acc_scr[...] / l).astype(BF16)

        odd = first + 2 * n_pairs < qi

        @pl.when(odd)
        def _():
            step(qi - 1, False)
            finalize()

        @pl.when(jnp.logical_not(odd))
        def _():
            finalize()

    def all_q_blocks(fixed_shift):
        def body(qi, carry):
            q_block(qi, fixed_shift)
            return carry
        lax.fori_loop(0, SEQ // t, body, 0)

    @pl.when(qk_bound < MAX_FIXED_SHIFT_BOUND)
    def _():
        all_q_blocks(True)

    @pl.when(qk_bound >= MAX_FIXED_SHIFT_BOUND)
    def _():
        all_q_blocks(False)


def _fox_attn(qkv, c3, cend3, par, cast_w=(), *, t=512):
    in_specs = [
        pl.BlockSpec(memory_space=pltpu.SMEM),
        pl.BlockSpec((SEQ, HEAD_DIM), lambda h: (0, h)),
        pl.BlockSpec((SEQ, HEAD_DIM), lambda h: (0, N_HEADS + h)),
        pl.BlockSpec((SEQ, HEAD_DIM), lambda h: (0, 2 * N_HEADS + h)),
        pl.BlockSpec((1, 1, SEQ), lambda h: (h, 0, 0)),
        pl.BlockSpec((1, 1, LANES), lambda h: (h, 0, 0)),
    ]
    out_specs = [pl.BlockSpec((SEQ, HEAD_DIM), lambda h: (0, h))]
    out_shape = [jax.ShapeDtypeStruct((SEQ, D_MODEL), BF16)]
    args = [par, qkv, qkv, qkv, c3, cend3]
    for w, layer, n_rows in cast_w:
        rows, cols = n_rows // N_HEADS, w.shape[2]
        assert rows % 16 == 0
        in_specs.append(pl.BlockSpec((None, rows, cols), lambda h, layer=layer: (layer, h, 0)))
        out_specs.append(pl.BlockSpec((rows, cols), lambda h: (h, 0)))
        out_shape.append(jax.ShapeDtypeStruct((n_rows, cols), BF16))
        args.append(w)
    return pl.pallas_call(
        functools.partial(_fox_attn_kernel, t=t, n_cast=len(cast_w)),
        grid=(N_HEADS,),
        in_specs=in_specs,
        out_specs=out_specs,
        out_shape=out_shape,
        scratch_shapes=[
            pltpu.VMEM((t, LANES), F32),
            pltpu.VMEM((t, LANES), F32),
            pltpu.VMEM((t, HEAD_DIM), F32),
        ],
        compiler_params=_params(("arbitrary",)),
        name="fox_attn",
    )(*args)


def _fox_out_kernel(x_ref, o_ref, w_ref, y_ref, wb_scr):
    @pl.when(pl.program_id(0) == 0)
    def _():
        wb_scr[...] = w_ref[...].astype(BF16)

    y_ref[...] = x_ref[...] + jnp.dot(o_ref[...], wb_scr[...],
                                      preferred_element_type=F32)


def _fox_out(x, o, w_out, layer, *, tm=512):
    return pl.pallas_call(
        _fox_out_kernel,
        grid=(SEQ // tm,),
        in_specs=[
            pl.BlockSpec((tm, D_MODEL), lambda i: (i, 0)),
            pl.BlockSpec((tm, D_MODEL), lambda i: (i, 0)),
            pl.BlockSpec((None, D_MODEL, D_MODEL), lambda i: (layer, 0, 0),
                         pipeline_mode=pl.Buffered(1)),
        ],
        out_specs=pl.BlockSpec((tm, D_MODEL), lambda i: (i, 0)),
        out_shape=jax.ShapeDtypeStruct((SEQ, D_MODEL), F32),
        scratch_shapes=[pltpu.VMEM((D_MODEL, D_MODEL), BF16)],
        compiler_params=_params(("arbitrary",)),
        name="fox_out",
    )(x, o, w_out)


def _pool_kernel(x_ref, g_ref, w_ref, b_ref, s_ref, y_ref, h_scr, a_scr, b_scr, wb_scr,
                 *, tm):
    i = pl.program_id(0)
    end = POOL_HALO + tm

    @pl.when(i == 0)
    def _():
        h_scr[0:POOL_HALO, :] = jnp.zeros((POOL_HALO, D_MODEL), F32)
        wb_scr[...] = w_ref[...].astype(BF16)

    x = x_ref[...]
    h_scr[POOL_HALO:end, :] = _rmsnorm(x, g_ref[...])
    t = i * tm + lax.broadcasted_iota(jnp.int32, (tm, 1), 0)
    for gi, win in enumerate(POOL_WINDOWS):
        cols = slice(gi * POOL_GROUP, (gi + 1) * POOL_GROUP)
        h = h_scr[POOL_HALO:end, cols]
        levels = win.bit_length() - 1
        src = lambda lo, hi: h_scr[lo:hi, cols]
        for k in range(1, levels + 1):
            lo = POOL_HALO - 8 * (levels - k)
            shift = 2 ** (k - 1)
            tot = src(lo, end) + src(lo - shift, end - shift)
            if k < levels:
                buf = a_scr if k % 2 else b_scr
                buf[lo:end, :] = tot
                src = lambda lo, hi, buf=buf: buf[lo:hi, :]
        cnt = jnp.minimum(t + 1, win).astype(F32)
        y = (tot / cnt - h).astype(BF16)
        y = jnp.dot(y, wb_scr[gi], preferred_element_type=F32)
        y_ref[:, cols] = x[:, cols] + (y + b_ref[:, cols]) * s_ref[:, cols]
    h_scr[0:POOL_HALO, :] = h_scr[tm:tm + POOL_HALO, :]


def _pool_mix(x, g, w, layer, b, scale, *, tm=512):
    n_groups = len(POOL_WINDOWS)
    return pl.pallas_call(
        functools.partial(_pool_kernel, tm=tm),
        grid=(SEQ // tm,),
        in_specs=[
            pl.BlockSpec((tm, D_MODEL), lambda i: (i, 0)),
            pl.BlockSpec((1, D_MODEL), lambda i: (0, 0)),
            pl.BlockSpec((None, n_groups, POOL_GROUP, POOL_GROUP),
                         lambda i: (layer, 0, 0, 0), pipeline_mode=pl.Buffered(1)),
            pl.BlockSpec((1, D_MODEL), lambda i: (0, 0)),
            pl.BlockSpec((1, D_MODEL), lambda i: (0, 0)),
        ],
        out_specs=pl.BlockSpec((tm, D_MODEL), lambda i: (i, 0)),
        out_shape=jax.ShapeDtypeStruct((SEQ, D_MODEL), F32),
        scratch_shapes=[pltpu.VMEM((POOL_HALO + tm, D_MODEL), F32),
                        pltpu.VMEM((POOL_HALO + tm, POOL_GROUP), F32),
                        pltpu.VMEM((POOL_HALO + tm, POOL_GROUP), F32),
                        pltpu.VMEM((n_groups, POOL_GROUP, POOL_GROUP), BF16)],
        compiler_params=_params(("arbitrary",)),
        name="pool_mix",
    )(x, g, w, b, scale)


FFN_ROW_CHUNK = 256


def _ffn_kernel(*refs, convert_next):
    if convert_next:
        (x_ref, g_ref, wg_ref, wu_ref, wd_ref, ngu_ref, ndn_ref,
         y_ref, ngu_out, ndn_out, h_scr) = refs
        ngu_out[...] = ngu_ref[...].astype(BF16)
        ndn_out[...] = ndn_ref[...].astype(BF16)
    else:
        x_ref, g_ref, wg_ref, wu_ref, wd_ref, y_ref, h_scr = refs
    j = pl.program_id(1)

    @pl.when(j == 0)
    def _():
        x = x_ref[...]
        h_scr[...] = _rmsnorm(x, g_ref[...]).astype(BF16)
        y_ref[...] = x

    chunks = [slice(r, r + FFN_ROW_CHUNK) for r in range(0, x_ref.shape[0], FFN_ROW_CHUNK)]
    acts = []
    for rows in chunks:
        h = h_scr[rows, :]
        gate = jnp.dot(h, wg_ref[...], preferred_element_type=F32)
        up = jnp.dot(h, wu_ref[...], preferred_element_type=F32)
        acts.append((gate * jax.nn.sigmoid(gate) * up).astype(BF16))
    for rows, a in zip(chunks, acts):
        y_ref[rows, :] += jnp.dot(a, wd_ref[...], preferred_element_type=F32)


def _ffn(x, g, w_gu, w_down, next_w=None, *, tm=1024, tf=512):
    nf = D_FF // tf
    steps = (SEQ // tm) * nf
    in_specs = [
        pl.BlockSpec((tm, D_MODEL), lambda i, j: (i, 0)),
        pl.BlockSpec((1, D_MODEL), lambda i, j: (0, 0)),
        pl.BlockSpec((D_MODEL, tf), lambda i, j: (0, j)),
        pl.BlockSpec((D_MODEL, tf), lambda i, j: (0, nf + j)),
        pl.BlockSpec((tf, D_MODEL), lambda i, j: (j, 0)),
    ]
    out_specs = [pl.BlockSpec((tm, D_MODEL), lambda i, j: (i, 0))]
    out_shape = [jax.ShapeDtypeStruct((SEQ, D_MODEL), F32)]
    args = [x, g, w_gu, w_gu, w_down]
    if next_w is not None:
        n_gu, n_down, layer = next_w
        gu_cols = 2 * D_FF // steps
        dn_rows = D_FF // steps
        assert gu_cols % LANES == 0 and dn_rows % 16 == 0
        in_specs += [
            pl.BlockSpec((None, D_MODEL, gu_cols), lambda i, j: (layer, 0, i * nf + j)),
            pl.BlockSpec((None, dn_rows, D_MODEL), lambda i, j: (layer, i * nf + j, 0)),
        ]
        out_specs += [
            pl.BlockSpec((D_MODEL, gu_cols), lambda i, j: (0, i * nf + j)),
            pl.BlockSpec((dn_rows, D_MODEL), lambda i, j: (i * nf + j, 0)),
        ]
        out_shape += [jax.ShapeDtypeStruct((D_MODEL, 2 * D_FF), BF16),
                      jax.ShapeDtypeStruct((D_FF, D_MODEL), BF16)]
        args += [n_gu, n_down]
    return pl.pallas_call(
        functools.partial(_ffn_kernel, convert_next=next_w is not None),
        grid=(SEQ // tm, nf),
        in_specs=in_specs,
        out_specs=out_specs,
        out_shape=out_shape,
        scratch_shapes=[pltpu.VMEM((tm, D_MODEL), BF16)],
        compiler_params=_params(("arbitrary", "arbitrary")),
        name="ffn",
    )(*args)


def kernel(x, mix_norm_g, ffn_norm_g, fox_w_in, fox_b_f, fox_q_norm_g, fox_k_norm_g,
           fox_w_out, pool_w, pool_b, pool_scale, ffn_w_gate_up, ffn_w_down):
    assert x.shape == (1, SEQ, D_MODEL)
    x = x.reshape(SEQ, D_MODEL)
    tri = jnp.tril(jnp.ones((LANES, LANES), BF16))
    q_scale = HEAD_DIM ** -0.5 * LOG2E
    fox_w_in_t = jnp.swapaxes(fox_w_in, 1, 2)
    w_gu = w_down = None
    for i in range(DEPTH):
        j = i // 2
        g_mix = mix_norm_g[i][None, :]
        if i % 2 == 0:
            wf_t = jnp.pad(fox_w_in_t[j, 3 * D_MODEL:, :], ((0, LANES - N_HEADS), (0, 0)))
            bf = jnp.pad(fox_b_f[j], (0, LANES - N_HEADS))[None, :]
            gains = jnp.concatenate([
                jnp.tile(fox_q_norm_g[j] * q_scale, N_HEADS),
                jnp.tile(fox_k_norm_g[j], N_HEADS),
                jnp.ones((D_MODEL,), F32)])[None, :]
            if i == 0:
                qkv, ct = _fox_proj(x, g_mix, fox_w_in_t, 0, gains, wf_t, bf, tri)
            else:
                qkv, ct = _fox_proj(x, g_mix, w_qkv_next, None, gains, wf_t, bf, tri)
            qk_bound = (1.02 * HEAD_DIM * q_scale * jnp.max(jnp.abs(fox_q_norm_g[j]))
                        * jnp.max(jnp.abs(fox_k_norm_g[j])))
            par = jnp.stack([-(SKIP_EXP2 + 2.0 * qk_bound), qk_bound]).astype(F32)
            cend = jnp.pad(ct[:, ATTN_BLOCK - 1::ATTN_BLOCK],
                           ((0, 0), (0, LANES - SEQ // ATTN_BLOCK)))
            if i == 0:
                o, w_gu, w_down, w_qkv_next = _fox_attn(
                    qkv, ct[:, None, :], cend[:, None, :], par,
                    [(ffn_w_gate_up, 0, D_MODEL), (ffn_w_down, 0, D_FF),
                     (fox_w_in_t, 1, 3 * D_MODEL)], t=ATTN_BLOCK)
            else:
                o, = _fox_attn(qkv, ct[:, None, :], cend[:, None, :], par, t=ATTN_BLOCK)
            x = _fox_out(x, o, fox_w_out, j)
        else:
            x = _pool_mix(x, g_mix, pool_w, j, pool_b[j][None, :], pool_scale[j][None, :])
        if i + 1 < DEPTH:
            x, w_gu, w_down = _ffn(x, ffn_norm_g[i][None, :], w_gu, w_down,
                                   (ffn_w_gate_up, ffn_w_down, i + 1))
        else:
            x, = _ffn(x, ffn_norm_g[i][None, :], w_gu, w_down)
    return x.reshape(1, SEQ, D_MODEL)
```

```python
import functools
import math

import jax
import jax.numpy as jnp
from jax import lax
from jax.experimental import pallas as pl
from jax.experimental.pallas import tpu as pltpu

D_MODEL = 2048
SEQ = 8192
DEPTH = 4
HEAD_DIM = 128
N_HEADS = D_MODEL // HEAD_DIM
POOL_WINDOWS = (2, 4, 8, 16)
POOL_GROUP = D_MODEL // len(POOL_WINDOWS)
POOL_HALO = 32
D_FF = 5632
RMS_EPS = 1e-6
NEG_INF = -1e30
LOG2E = math.log2(math.e)

LANES = 128
VMEM_LIMIT = 62 * 1024 * 1024

F32 = jnp.float32
BF16 = jnp.bfloat16


def _params(semantics):
    return pltpu.CompilerParams(dimension_semantics=semantics,
                                vmem_limit_bytes=VMEM_LIMIT)


def _rmsnorm(x, g):
    return x * lax.rsqrt(jnp.mean(x * x, axis=-1, keepdims=True) + RMS_EPS) * g


def _split3_bf16(x):
    x1 = x.astype(BF16)
    r1 = x - x1.astype(F32)
    x2 = r1.astype(BF16)
    x3 = (r1 - x2.astype(F32)).astype(BF16)
    return x1, x2, x3


ROW_CHUNK = 256
_NT_DIMS = (((1,), (1,)), ((), ()))


def _fox_proj_kernel(x_ref, g_ref, w_ref, gain_ref, wf_ref, bf_ref, tri_ref,
                     qkv_ref, ct_ref, h_scr, carry_scr, *, tm, tn):
    i = pl.program_id(0)
    j = pl.program_id(1)

    @pl.when(j == 0)
    def _():
        hb = _rmsnorm(x_ref[...], g_ref[...]).astype(BF16)
        h_scr[...] = hb
        f = lax.dot_general(hb, wf_ref[...].astype(BF16), _NT_DIMS,
                            preferred_element_type=F32) + bf_ref[...]
        logf = jnp.minimum(f, 0.0) - jnp.log1p(jnp.exp(-jnp.abs(f)))

        @pl.when(i == 0)
        def _():
            carry_scr[...] = jnp.zeros_like(carry_scr)

        tri = tri_ref[...]
        carry = carry_scr[0:1, :]
        chunks = []
        for r in range(tm // LANES):
            x1, x2, x3 = _split3_bf16(logf[r * LANES:(r + 1) * LANES, :])
            cs = (jnp.dot(tri, x1, preferred_element_type=F32)
                  + jnp.dot(tri, x2, preferred_element_type=F32)
                  + jnp.dot(tri, x3, preferred_element_type=F32)) + carry
            carry = cs[LANES - 1:LANES, :]
            chunks.append(cs)
        carry_scr[0:1, :] = carry
        c = jnp.concatenate(chunks, axis=0) * LOG2E
        ct_ref[...] = c.T[:N_HEADS, :]

    w_t = w_ref[...].astype(BF16)
    is_qk = (j < 2 * D_MODEL // tn).astype(F32)
    for rc in range(tm // ROW_CHUNK):
        rows = slice(rc * ROW_CHUNK, (rc + 1) * ROW_CHUNK)
        r = lax.dot_general(h_scr[rows, :], w_t, _NT_DIMS,
                            preferred_element_type=F32)
        for hh in range(tn // HEAD_DIM):
            sl = slice(hh * HEAD_DIM, (hh + 1) * HEAD_DIM)
            rh = r[:, sl]
            inv = lax.rsqrt(jnp.mean(rh * rh, axis=-1, keepdims=True) + RMS_EPS)
            scale = is_qk * inv + (1.0 - is_qk)
            qkv_ref[rows, sl] = (rh * scale * gain_ref[:, sl]).astype(BF16)


def _fox_proj(x, g, w_in_t, layer, gains, wf_t, bf, tri, *, tm=1024, tn=1024):
    n = 3 * D_MODEL
    return pl.pallas_call(
        functools.partial(_fox_proj_kernel, tm=tm, tn=tn),
        grid=(SEQ // tm, n // tn),
        in_specs=[
            pl.BlockSpec((tm, D_MODEL), lambda i, j: (i, 0)),
            pl.BlockSpec((1, D_MODEL), lambda i, j: (0, 0)),
            pl.BlockSpec((None, tn, D_MODEL), lambda i, j: (layer, j, 0)),
            pl.BlockSpec((1, tn), lambda i, j: (0, j)),
            pl.BlockSpec((LANES, D_MODEL), lambda i, j: (0, 0)),
            pl.BlockSpec((1, LANES), lambda i, j: (0, 0)),
            pl.BlockSpec((LANES, LANES), lambda i, j: (0, 0)),
        ],
        out_specs=[
            pl.BlockSpec((tm, tn), lambda i, j: (i, j)),
            pl.BlockSpec((N_HEADS, tm), lambda i, j: (0, i)),
        ],
        out_shape=[
            jax.ShapeDtypeStruct((SEQ, n), BF16),
            jax.ShapeDtypeStruct((N_HEADS, SEQ), F32),
        ],
        scratch_shapes=[
            pltpu.VMEM((tm, D_MODEL), BF16),
            pltpu.VMEM((8, LANES), F32),
        ],
        compiler_params=_params(("arbitrary", "arbitrary")),
        name="fox_proj",
    )(x, g, w_in_t, gains, wf_t, bf, tri)


SKIP_EXP2 = 151.0
ATTN_BLOCK = 512
MAX_FIXED_SHIFT_BOUND = 48.0


def _fox_attn_kernel(*refs, t, convert):
    if convert:
        (par_ref, q_ref, k_ref, v_ref, c_ref, cend_ref, wa_ref, wb_ref,
         o_ref, wa_out, wb_out, m_scr, l_scr, acc_scr) = refs
        wa_out[...] = wa_ref[...].astype(BF16)
        wb_out[...] = wb_ref[...].astype(BF16)
    else:
        (par_ref, q_ref, k_ref, v_ref, c_ref, cend_ref,
         o_ref, m_scr, l_scr, acc_scr) = refs
    blk = lax.broadcasted_iota(jnp.int32, (1, LANES), 1)
    cend = cend_ref[0]
    thr = par_ref[0]
    qk_bound = par_ref[1]

    def causal(x):
        row = lax.broadcasted_iota(jnp.int32, (t, t), 0)
        col = lax.broadcasted_iota(jnp.int32, (t, t), 1)
        return jnp.where(row >= col, x, NEG_INF)

    def scores(q, c_q0, j):
        k0 = pl.multiple_of(j * t, t)
        s = lax.dot_general(q, k_ref[pl.ds(k0, t), :], _NT_DIMS,
                            preferred_element_type=F32)
        return s + (c_q0 - c_ref[0, :, pl.ds(k0, t)])

    def v_block(j):
        return v_ref[pl.ds(pl.multiple_of(j * t, t), t), :]

    def q_block(qi, fixed_shift):
        q0 = pl.multiple_of(qi * t, t)
        q = q_ref[pl.ds(q0, t), :]
        c_q = c_ref[0, :, pl.ds(q0, t)]
        c_q0 = c_q[:, 0:1]
        skippable = ((c_q0 - cend) < thr) & (blk < qi)
        first = jnp.sum(skippable.astype(jnp.int32))
        l_scr[...] = jnp.zeros_like(l_scr)
        acc_scr[...] = jnp.zeros_like(acc_scr)

        if fixed_shift:
            shift = qk_bound + (c_q0 - c_q)
            shift = jnp.broadcast_to(shift, (LANES, t)).T
            shift = jnp.concatenate([shift] * (t // LANES), axis=1)

            def step(j, masked):
                x = scores(q, c_q0, j) - shift
                p = jnp.exp2(causal(x) if masked else x)
                part = p[:, 0:LANES]
                for n in range(1, t // LANES):
                    part = part + p[:, n * LANES:(n + 1) * LANES]
                l_scr[...] += part
                acc_scr[...] += jnp.dot(p.astype(BF16), v_block(j),
                                        preferred_element_type=F32)
        else:
            m_scr[...] = jnp.full_like(m_scr, NEG_INF)

            def step(j, masked):
                s = scores(q, c_q0, j)
                if masked:
                    s = causal(s)
                m_prev = m_scr[...]
                m_next = jnp.maximum(m_prev, jnp.max(s, axis=1, keepdims=True))
                p = jnp.exp2(s - jnp.concatenate([m_next] * (t // LANES), axis=1))
                alpha = jnp.exp2(m_prev - m_next)
                l_scr[...] = alpha * l_scr[...] + jnp.sum(p, axis=1, keepdims=True)
                m_scr[...] = m_next
                acc_scr[...] = alpha * acc_scr[...] + jnp.dot(
                    p.astype(BF16), v_block(j), preferred_element_type=F32)

        step(qi, True)

        n_before = jnp.maximum(qi - first - 1, 0)
        n_pairs = n_before // 2

        def kv_pair(n, c):
            j = first + 2 * n
            step(j, False)
            step(j + 1, False)
            return c

        lax.fori_loop(0, n_pairs, kv_pair, 0)

        @pl.when(n_before - 2 * n_pairs == 1)
        def _():
            step(first + 2 * n_pairs, False)

        def finalize():
            l = l_scr[...]
            if fixed_shift:
                l = jnp.sum(l, axis=1, keepdims=True)
            o_ref[pl.ds(q0, t), :] = (acc_scr[...] / l).astype(BF16)

        has_tail = first < qi

        @pl.when(has_tail)
        def _():
            step(qi - 1, False)
            finalize()

        @pl.when(jnp.logical_not(has_tail))
        def _():
            finalize()

    def all_q_blocks(fixed_shift):
        def body(qi, carry):
            q_block(qi, fixed_shift)
            return carry
        lax.fori_loop(0, SEQ // t, body, 0)

    @pl.when(qk_bound < MAX_FIXED_SHIFT_BOUND)
    def _():
        all_q_blocks(True)

    @pl.when(qk_bound >= MAX_FIXED_SHIFT_BOUND)
    def _():
        all_q_blocks(False)


def _fox_attn(qkv, c3, cend3, par, cast_w=None, *, t=512):
    in_specs = [
        pl.BlockSpec(memory_space=pltpu.SMEM),
        pl.BlockSpec((SEQ, HEAD_DIM), lambda h: (0, h)),
        pl.BlockSpec((SEQ, HEAD_DIM), lambda h: (0, N_HEADS + h)),
        pl.BlockSpec((SEQ, HEAD_DIM), lambda h: (0, 2 * N_HEADS + h)),
        pl.BlockSpec((1, 1, SEQ), lambda h: (h, 0, 0)),
        pl.BlockSpec((1, 1, LANES), lambda h: (h, 0, 0)),
    ]
    out_specs = [pl.BlockSpec((SEQ, HEAD_DIM), lambda h: (0, h))]
    out_shape = [jax.ShapeDtypeStruct((SEQ, D_MODEL), BF16)]
    args = [par, qkv, qkv, qkv, c3, cend3]
    if cast_w is not None:
        wa, wb, layer = cast_w
        for w in (wa, wb):
            rows, cols = w.shape[1] // N_HEADS, w.shape[2]
            assert rows % 16 == 0
            in_specs.append(pl.BlockSpec((None, rows, cols), lambda h: (layer, h, 0)))
            out_specs.append(pl.BlockSpec((rows, cols), lambda h: (h, 0)))
            out_shape.append(jax.ShapeDtypeStruct(w.shape[1:], BF16))
        args += [wa, wb]
    return pl.pallas_call(
        functools.partial(_fox_attn_kernel, t=t, convert=cast_w is not None),
        grid=(N_HEADS,),
        in_specs=in_specs,
        out_specs=out_specs,
        out_shape=out_shape,
        scratch_shapes=[
            pltpu.VMEM((t, LANES), F32),
            pltpu.VMEM((t, LANES), F32),
            pltpu.VMEM((t, HEAD_DIM), F32),
        ],
        compiler_params=_params(("arbitrary",)),
        name="fox_attn",
    )(*args)


def _fox_out_kernel(x_ref, o_ref, w_ref, y_ref, wb_scr):
    @pl.when(pl.program_id(0) == 0)
    def _():
        wb_scr[...] = w_ref[...].astype(BF16)

    y_ref[...] = x_ref[...] + jnp.dot(o_ref[...], wb_scr[...],
                                      preferred_element_type=F32)


def _fox_out(x, o, w_out, layer, *, tm=512):
    return pl.pallas_call(
        _fox_out_kernel,
        grid=(SEQ // tm,),
        in_specs=[
            pl.BlockSpec((tm, D_MODEL), lambda i: (i, 0)),
            pl.BlockSpec((tm, D_MODEL), lambda i: (i, 0)),
            pl.BlockSpec((None, D_MODEL, D_MODEL), lambda i: (layer, 0, 0),
                         pipeline_mode=pl.Buffered(1)),
        ],
        out_specs=pl.BlockSpec((tm, D_MODEL), lambda i: (i, 0)),
        out_shape=jax.ShapeDtypeStruct((SEQ, D_MODEL), F32),
        scratch_shapes=[pltpu.VMEM((D_MODEL, D_MODEL), BF16)],
        compiler_params=_params(("arbitrary",)),
        name="fox_out",
    )(x, o, w_out)


def _pool_kernel(x_ref, g_ref, w_ref, b_ref, s_ref, y_ref, h_scr, a_scr, b_scr, wb_scr,
                 *, tm):
    i = pl.program_id(0)
    end = POOL_HALO + tm

    @pl.when(i == 0)
    def _():
        h_scr[0:POOL_HALO, :] = jnp.zeros((POOL_HALO, D_MODEL), F32)
        wb_scr[...] = w_ref[...].astype(BF16)

    x = x_ref[...]
    h_scr[POOL_HALO:end, :] = _rmsnorm(x, g_ref[...])
    t = i * tm + lax.broadcasted_iota(jnp.int32, (tm, 1), 0)
    for gi, win in enumerate(POOL_WINDOWS):
        cols = slice(gi * POOL_GROUP, (gi + 1) * POOL_GROUP)
        h = h_scr[POOL_HALO:end, cols]
        levels = win.bit_length() - 1
        src = lambda lo, hi: h_scr[lo:hi, cols]
        for k in range(1, levels + 1):
            lo = POOL_HALO - 8 * (levels - k)
            shift = 2 ** (k - 1)
            tot = src(lo, end) + src(lo - shift, end - shift)
            if k < levels:
                buf = a_scr if k % 2 else b_scr
                buf[lo:end, :] = tot
                src = lambda lo, hi, buf=buf: buf[lo:hi, :]
        cnt = jnp.minimum(t + 1, win).astype(F32)
        y = (tot / cnt - h).astype(BF16)
        y = jnp.dot(y, wb_scr[gi], preferred_element_type=F32)
        y_ref[:, cols] = x[:, cols] + (y + b_ref[:, cols]) * s_ref[:, cols]
    h_scr[0:POOL_HALO, :] = h_scr[tm:tm + POOL_HALO, :]


def _pool_mix(x, g, w, layer, b, scale, *, tm=512):
    n_groups = len(POOL_WINDOWS)
    return pl.pallas_call(
        functools.partial(_pool_kernel, tm=tm),
        grid=(SEQ // tm,),
        in_specs=[
            pl.BlockSpec((tm, D_MODEL), lambda i: (i, 0)),
            pl.BlockSpec((1, D_MODEL), lambda i: (0, 0)),
            pl.BlockSpec((None, n_groups, POOL_GROUP, POOL_GROUP),
                         lambda i: (layer, 0, 0, 0), pipeline_mode=pl.Buffered(1)),
            pl.BlockSpec((1, D_MODEL), lambda i: (0, 0)),
            pl.BlockSpec((1, D_MODEL), lambda i: (0, 0)),
        ],
        out_specs=pl.BlockSpec((tm, D_MODEL), lambda i: (i, 0)),
        out_shape=jax.ShapeDtypeStruct((SEQ, D_MODEL), F32),
        scratch_shapes=[pltpu.VMEM((POOL_HALO + tm, D_MODEL), F32),
                        pltpu.VMEM((POOL_HALO + tm, POOL_GROUP), F32),
                        pltpu.VMEM((POOL_HALO + tm, POOL_GROUP), F32),
                        pltpu.VMEM((n_groups, POOL_GROUP, POOL_GROUP), BF16)],
        compiler_params=_params(("arbitrary",)),
        name="pool_mix",
    )(x, g, w, b, scale)


FFN_ROW_CHUNK = 256


def _ffn_kernel(*refs, convert_next):
    if convert_next:
        (x_ref, g_ref, wg_ref, wu_ref, wd_ref, ngu_ref, ndn_ref,
         y_ref, ngu_out, ndn_out, h_scr) = refs
        ngu_out[...] = ngu_ref[...].astype(BF16)
        ndn_out[...] = ndn_ref[...].astype(BF16)
    else:
        x_ref, g_ref, wg_ref, wu_ref, wd_ref, y_ref, h_scr = refs
    j = pl.program_id(1)

    @pl.when(j == 0)
    def _():
        x = x_ref[...]
        h_scr[...] = _rmsnorm(x, g_ref[...]).astype(BF16)
        y_ref[...] = x

    chunks = [slice(r, r + FFN_ROW_CHUNK) for r in range(0, x_ref.shape[0], FFN_ROW_CHUNK)]
    acts = []
    for rows in chunks:
        h = h_scr[rows, :]
        gate = jnp.dot(h, wg_ref[...], preferred_element_type=F32)
        up = jnp.dot(h, wu_ref[...], preferred_element_type=F32)
        acts.append((gate * jax.nn.sigmoid(gate) * up).astype(BF16))
    for rows, a in zip(chunks, acts):
        y_ref[rows, :] += jnp.dot(a, wd_ref[...], preferred_element_type=F32)


def _ffn(x, g, w_gu, w_down, next_w=None, *, tm=1024, tf=512):
    nf = D_FF // tf
    steps = (SEQ // tm) * nf
    in_specs = [
        pl.BlockSpec((tm, D_MODEL), lambda i, j: (i, 0)),
        pl.BlockSpec((1, D_MODEL), lambda i, j: (0, 0)),
        pl.BlockSpec((D_MODEL, tf), lambda i, j: (0, j)),
        pl.BlockSpec((D_MODEL, tf), lambda i, j: (0, nf + j)),
        pl.BlockSpec((tf, D_MODEL), lambda i, j: (j, 0)),
    ]
    out_specs = [pl.BlockSpec((tm, D_MODEL), lambda i, j: (i, 0))]
    out_shape = [jax.ShapeDtypeStruct((SEQ, D_MODEL), F32)]
    args = [x, g, w_gu, w_gu, w_down]
    if next_w is not None:
        n_gu, n_down, layer = next_w
        gu_cols = 2 * D_FF // steps
        dn_rows = D_FF // steps
        assert gu_cols % LANES == 0 and dn_rows % 16 == 0
        in_specs += [
            pl.BlockSpec((None, D_MODEL, gu_cols), lambda i, j: (layer, 0, i * nf + j)),
            pl.BlockSpec((None, dn_rows, D_MODEL), lambda i, j: (layer, i * nf + j, 0)),
        ]
        out_specs += [
            pl.BlockSpec((D_MODEL, gu_cols), lambda i, j: (0, i * nf + j)),
            pl.BlockSpec((dn_rows, D_MODEL), lambda i, j: (i * nf + j, 0)),
        ]
        out_shape += [jax.ShapeDtypeStruct((D_MODEL, 2 * D_FF), BF16),
                      jax.ShapeDtypeStruct((D_FF, D_MODEL), BF16)]
        args += [n_gu, n_down]
    return pl.pallas_call(
        functools.partial(_ffn_kernel, convert_next=next_w is not None),
        grid=(SEQ // tm, nf),
        in_specs=in_specs,
        out_specs=out_specs,
        out_shape=out_shape,
        scratch_shapes=[pltpu.VMEM((tm, D_MODEL), BF16)],
        compiler_params=_params(("arbitrary", "arbitrary")),
        name="ffn",
    )(*args)


def kernel(x, mix_norm_g, ffn_norm_g, fox_w_in, fox_b_f, fox_q_norm_g, fox_k_norm_g,
           fox_w_out, pool_w, pool_b, pool_scale, ffn_w_gate_up, ffn_w_down):
    assert x.shape == (1, SEQ, D_MODEL)
    x = x.reshape(SEQ, D_MODEL)
    tri = jnp.tril(jnp.ones((LANES, LANES), BF16))
    q_scale = HEAD_DIM ** -0.5 * LOG2E
    fox_w_in_t = jnp.swapaxes(fox_w_in, 1, 2)
    w_gu = w_down = None
    for i in range(DEPTH):
        j = i // 2
        g_mix = mix_norm_g[i][None, :]
        if i % 2 == 0:
            wf_t = jnp.pad(fox_w_in_t[j, 3 * D_MODEL:, :], ((0, LANES - N_HEADS), (0, 0)))
            bf = jnp.pad(fox_b_f[j], (0, LANES - N_HEADS))[None, :]
            gains = jnp.concatenate([
                jnp.tile(fox_q_norm_g[j] * q_scale, N_HEADS),
                jnp.tile(fox_k_norm_g[j], N_HEADS),
                jnp.ones((D_MODEL,), F32)])[None, :]
            qkv, ct = _fox_proj(x, g_mix, fox_w_in_t, j, gains, wf_t, bf, tri)
            qk_bound = (1.02 * HEAD_DIM * q_scale * jnp.max(jnp.abs(fox_q_norm_g[j]))
                        * jnp.max(jnp.abs(fox_k_norm_g[j])))
            par = jnp.stack([-(SKIP_EXP2 + 2.0 * qk_bound), qk_bound]).astype(F32)
            cend = jnp.pad(ct[:, ATTN_BLOCK - 1::ATTN_BLOCK],
                           ((0, 0), (0, LANES - SEQ // ATTN_BLOCK)))
            if i == 0:
                o, w_gu, w_down = _fox_attn(qkv, ct[:, None, :], cend[:, None, :], par,
                                            (ffn_w_gate_up, ffn_w_down, 0), t=ATTN_BLOCK)
            else:
                o, = _fox_attn(qkv, ct[:, None, :], cend[:, None, :], par, t=ATTN_BLOCK)
            x = _fox_out(x, o, fox_w_out, j)
        else:
            x = _pool_mix(x, g_mix, pool_w, j, pool_b[j][None, :], pool_scale[j][None, :])
        if i + 1 < DEPTH:
            x, w_gu, w_down = _ffn(x, ffn_norm_g[i][None, :], w_gu, w_down,
                                   (ffn_w_gate_up, ffn_w_down, i + 1))
        else:
            x, = _ffn(x, ffn_norm_g[i][None, :], w_gu, w_down)
    return x.reshape(1, SEQ, D_MODEL)
```
